```python
import jax, jax.numpy as jnp
from jax import lax
import numpy as np

D_MODEL = 1024
BATCH = 16
SEQ = 256
DEPTH = 2
DEC_BATCH = 8
DEC_SEQ = 4096
PAST_LEN = 256

F32 = jnp.float32
GRID_W = 64
N_BRANCH = 4
BRANCH_W = D_MODEL // 2
HG_DK = 128
HG_DV = 128
HG_HEADS = BRANCH_W // HG_DK
HG_CHUNK = 64
NA_DH = 64
NA_HEADS = BRANCH_W // NA_DH
WIN_R = 8
WIN_C = 16
NOPE_DIM = 128
ROPE_DIM = 64
V_DIM = 128
MLA_HEADS = BRANCH_W // V_DIM
Q_RANK = 256
KV_RANK = 128
ROPE_THETA = 10000.0
CONV_W = 3
Q_BLOCK = 128
EPS = 1e-6
DEEPNORM_ALPHA = (2 * DEPTH) ** 0.25
DEEPNORM_BETA = (8 * DEPTH) ** -0.25

SPLIT_SIZES = (
    BRANCH_W, BRANCH_W, BRANCH_W, BRANCH_W, BRANCH_W,
    BRANCH_W, BRANCH_W, BRANCH_W, BRANCH_W,
    Q_RANK, KV_RANK, ROPE_DIM, BRANCH_W,
    BRANCH_W, BRANCH_W, BRANCH_W, BRANCH_W,
    N_BRANCH * D_MODEL,
)
IN_W = sum(SPLIT_SIZES)

kernel_name = 'hybrid_diffusion_parallel_branch_step'


def split_columns(z):
    points = np.cumsum(np.array(SPLIT_SIZES))[:-1].tolist()
    return jnp.split(z, points, axis=-1)


def layer_norm(x, g, b):
    xf = x.astype(F32)
    mu = jnp.mean(xf, axis=-1, keepdims=True)
    var = jnp.mean(jnp.square(xf - mu), axis=-1, keepdims=True)
    return ((xf - mu) * lax.rsqrt(var + EPS)).astype(x.dtype) * g + b


def rms_norm(x, g):
    xf = x.astype(F32)
    y = xf * lax.rsqrt(jnp.mean(jnp.square(xf), axis=-1, keepdims=True) + EPS)
    return y.astype(x.dtype) * g


def axial_rope(x):
    T = x.shape[1]
    t = jnp.arange(T)
    row = (t // GRID_W).astype(F32)
    col = (t % GRID_W).astype(F32)
    n_pair_axis = ROPE_DIM // 4
    inv = 1.0 / (ROPE_THETA ** (jnp.arange(n_pair_axis, dtype=F32) / n_pair_axis))
    ang = jnp.concatenate([row[:, None] * inv, col[:, None] * inv], axis=-1)
    cos = jnp.cos(ang)[None, :, None, :].astype(x.dtype)
    sin = jnp.sin(ang)[None, :, None, :].astype(x.dtype)
    x1, x2 = x[..., :ROPE_DIM // 2], x[..., ROPE_DIM // 2:]
    return jnp.concatenate([x1 * cos - x2 * sin, x1 * sin + x2 * cos], axis=-1)


def blocked_attention(q, k, v, scale):
    B, Tq, H, dq = q.shape
    qb = jnp.moveaxis(q.reshape(B, Tq // Q_BLOCK, Q_BLOCK, H, dq), 1, 0)

    def attend(q_blk):
        s = jnp.einsum('bqhd,bkhd->bhqk', q_blk, k).astype(F32) * scale
        p = jax.nn.softmax(s, axis=-1).astype(v.dtype)
        return jnp.einsum('bhqk,bkhe->bqhe', p, v)

    o = lax.map(attend, qb)
    return jnp.moveaxis(o, 0, 1).reshape(B, Tq, H, v.shape[-1])


def neighbourhood_attention(q, k, v, ck, cv, rpb):
    B, T, H, d = q.shape
    rows = T // GRID_W
    wr = min(WIN_R, rows)
    scale = d ** -0.5
    q5 = q.reshape(B, rows, GRID_W, H, d)
    k5 = k.reshape(B, rows, GRID_W, H, d)
    v5 = v.reshape(B, rows, GRID_W, H, d)
    cols = jnp.arange(GRID_W)
    col_start = jnp.clip(cols - WIN_C // 2, 0, GRID_W - WIN_C)
    col_idx = col_start[:, None] + jnp.arange(WIN_C)[None, :]
    col_off = col_idx - cols[:, None] + (WIN_C - 1)

    def row_block(r):
        rs = jnp.clip(r - wr // 2, 0, rows - wr)
        qr = lax.dynamic_index_in_dim(q5, r, axis=1, keepdims=False)
        kg = lax.dynamic_slice_in_dim(k5, rs, wr, axis=1)[:, :, col_idx]
        vg = lax.dynamic_slice_in_dim(v5, rs, wr, axis=1)[:, :, col_idx]
        row_off = rs + jnp.arange(wr) - r + (WIN_R - 1)
        bias = jnp.transpose(rpb[:, row_off][:, :, col_off], (0, 2, 1, 3))
        s_loc = jnp.einsum('bqhd,brqchd->bhqrc', qr, kg).astype(F32) * scale + bias[None].astype(F32)
        s_ctx = jnp.einsum('bqhd,bkhd->bhqk', qr, ck).astype(F32) * scale
        n_loc = wr * WIN_C
        s = jnp.concatenate([s_loc.reshape(B, H, GRID_W, n_loc), s_ctx], axis=-1)
        p = jax.nn.softmax(s, axis=-1).astype(v.dtype)
        p_loc = p[..., :n_loc].reshape(B, H, GRID_W, wr, WIN_C)
        return (jnp.einsum('bhqrc,brqchd->bqhd', p_loc, vg)
                + jnp.einsum('bhqk,bkhd->bqhd', p[..., n_loc:], cv))

    out = lax.map(row_block, jnp.arange(rows))
    return jnp.moveaxis(out, 0, 1).reshape(B, T, H, d)


def hgrn_gates(f_raw, lb):
    log_g = jnp.logaddexp(jnp.log(lb), jnp.log1p(-lb) + jax.nn.log_sigmoid(f_raw.astype(F32)))
    return log_g, -jnp.expm1(log_g)


def chunk_gla(q, k, v, log_g, s0):
    B, T, H, dk = q.shape
    dv = v.shape[-1]
    L = HG_CHUNK
    n = T // L

    def to_chunks(a):
        return jnp.transpose(a.astype(F32).reshape(B, n, L, H, a.shape[-1]), (1, 0, 3, 2, 4))

    causal = jnp.tril(jnp.ones((L, L), dtype=bool))

    def step(S, inp):
        qc, kc, vc, gc = inp
        b = jnp.cumsum(gc, axis=2)
        diff = b[:, :, :, None, :] - b[:, :, None, :, :]
        decay = jnp.exp(jnp.where(causal[:, :, None], diff, -jnp.inf))
        attn = jnp.einsum('bhtd,bhsd,bhtsd->bhts', qc, kc, decay)
        o = (jnp.einsum('bhts,bhse->bhte', attn, vc)
             + jnp.einsum('bhtd,bhde->bhte', qc * jnp.exp(b), S))
        b_last = b[:, :, -1:, :]
        S = (jnp.exp(b_last[:, :, 0, :, None]) * S
             + jnp.einsum('bhsd,bhse->bhde', kc * jnp.exp(b_last - b), vc))
        return S, o

    s_fin, o = lax.scan(step, s0.astype(F32),
                        (to_chunks(q), to_chunks(k), to_chunks(v), to_chunks(log_g)))
    return jnp.transpose(o, (1, 0, 3, 2, 4)).reshape(B, T, H, dv), s_fin


def hgrn2_branch(q_raw, ff_raw, fb_raw, i_raw, g_raw, lb_f, lb_b, norm_g, s0_f, s0_b):
    B, T, _ = q_raw.shape

    def heads(a, d):
        return a.reshape(B, T, HG_HEADS, d)

    def rev(a):
        return jnp.flip(a, axis=1)

    q = heads(jax.nn.silu(q_raw.astype(F32)), HG_DK)
    v = heads(i_raw.astype(F32), HG_DV)
    lg_f, k_f = hgrn_gates(ff_raw, lb_f)
    lg_b, k_b = hgrn_gates(fb_raw, lb_b)
    o_f, s_f = chunk_gla(q, heads(k_f, HG_DK), v, heads(lg_f, HG_DK), s0_f)
    o_b, s_b = chunk_gla(rev(q), rev(heads(k_b, HG_DK)), rev(v), rev(heads(lg_b, HG_DK)), s0_b)
    o = rms_norm(o_f + rev(o_b), norm_g).reshape(B, T, BRANCH_W).astype(g_raw.dtype)
    return o * jax.nn.silu(g_raw), s_f, s_b


def mla_expand_kv(ckv, w_kvb):
    B, T, _ = ckv.shape
    kv = (ckv @ w_kvb).reshape(B, T, MLA_HEADS, NOPE_DIM + V_DIM)
    return kv[..., :NOPE_DIM], kv[..., NOPE_DIM:]


def mla_keys(k_nope, kpe):
    kpe_h = jnp.broadcast_to(kpe[:, :, None, :], k_nope.shape[:3] + (ROPE_DIM,))
    return jnp.concatenate([k_nope, kpe_h], axis=-1)


def short_conv(u, w):
    return lax.conv_general_dilated(u, w[:, None, :], window_strides=(1,),
                                    padding=((CONV_W // 2, CONV_W // 2),),
                                    dimension_numbers=('NWC', 'WIO', 'NWC'),
                                    feature_group_count=u.shape[-1])


def trunk_layer(x, cvec, ctx_cache, w_ada, b_ada, w_in, b_in, lb_f, lb_b, hg_norm_g, na_rpb,
                mla_qnorm_g, mla_w_qb, mla_kvnorm_g, mla_w_kvb, conv_w, w_branch, w_out, ln_g, ln_b):
    B, T, _ = x.shape
    is_ctx = ctx_cache is None
    mod = (jax.nn.silu(cvec) @ w_ada + b_ada).reshape(-1, 1, 3 * D_MODEL)
    shift, scale, gate = jnp.split(mod, 3, axis=-1)
    h = x * (1 + scale) + shift
    (a_q, a_ff, a_fb, a_i, a_g, b_q, b_k, b_v, b_g, c_qd, c_kvd, c_kpe, c_g,
     d_b, d_c, d_x, d_g, merge) = split_columns(h @ w_in + b_in)

    if is_ctx:
        s0 = jnp.zeros((B, 2, HG_HEADS, HG_DK, HG_DV), F32)
    else:
        s0 = ctx_cache[0].astype(F32)
    out_a, s_f, s_b = hgrn2_branch(a_q, a_ff, a_fb, a_i, a_g, lb_f, lb_b, hg_norm_g,
                                   s0[:, 0], s0[:, 1])

    q_na = b_q.reshape(B, T, NA_HEADS, NA_DH)
    k_na = b_k.reshape(B, T, NA_HEADS, NA_DH)
    v_na = b_v.reshape(B, T, NA_HEADS, NA_DH)
    if is_ctx:
        o_na = blocked_attention(q_na, k_na, v_na, NA_DH ** -0.5)
    else:
        o_na = neighbourhood_attention(q_na, k_na, v_na, ctx_cache[1], ctx_cache[2], na_rpb)
    out_b = o_na.reshape(B, T, BRANCH_W) * jax.nn.silu(b_g)

    q_mla = (rms_norm(c_qd, mla_qnorm_g) @ mla_w_qb).reshape(B, T, MLA_HEADS, NOPE_DIM + ROPE_DIM)
    ckv = rms_norm(c_kvd, mla_kvnorm_g)
    k_nope, v_mla = mla_expand_kv(ckv, mla_w_kvb)
    if is_ctx:
        keys = mla_keys(k_nope, c_kpe)
        vals = v_mla
    else:
        q_mla = jnp.concatenate([q_mla[..., :NOPE_DIM], axial_rope(q_mla[..., NOPE_DIM:])], axis=-1)
        kpe_lat = axial_rope(c_kpe[:, :, None, :])[:, :, 0]
        kn_ctx, v_ctx = mla_expand_kv(ctx_cache[3], mla_w_kvb)
        keys = jnp.concatenate([mla_keys(k_nope, kpe_lat), mla_keys(kn_ctx, ctx_cache[4])], axis=1)
        vals = jnp.concatenate([v_mla, v_ctx], axis=1)
    o_mla = blocked_attention(q_mla, keys, vals, (NOPE_DIM + ROPE_DIM) ** -0.5)
    out_c = o_mla.reshape(B, T, BRANCH_W) * jax.nn.silu(c_g)

    out_d = d_b * short_conv(d_c * d_x, conv_w) * jax.nn.silu(d_g)

    branches = jnp.stack([out_a, out_b, out_c, out_d], axis=2)
    proj = jnp.einsum('btnw,nwd->btnd', branches, w_branch)
    gates = jax.nn.sigmoid(merge).reshape(B, T, N_BRANCH, D_MODEL)
    mixed = jnp.einsum('btnd,btnd->btd', gates, proj) @ w_out
    y = layer_norm(DEEPNORM_ALPHA * x + gate * mixed, ln_g, ln_b)
    if is_ctx:
        states = jnp.stack([s_f, s_b], axis=1).astype(x.dtype)
        return y, (states, k_na, v_na, ckv, c_kpe)
    return y, None


def setup_inputs(seed: int = 0) -> dict:
    key = jax.random.key(seed)
    ks = jax.random.split(key, 26)

    def nrm(i, shape, s):
        return jax.random.normal(ks[i], shape, F32) * s

    return {
        'x_prompt': nrm(0, (BATCH, SEQ, D_MODEL), 1.0),
        'x_sample': nrm(1, (DEC_BATCH, DEC_SEQ, D_MODEL), 1.0),
        'state_hgrn': nrm(2, (DEC_BATCH, DEPTH, 2, HG_HEADS, HG_DK, HG_DV), 0.5),
        'cache_na_k': nrm(3, (DEC_BATCH, DEPTH, PAST_LEN, NA_HEADS, NA_DH), 1.0),
        'cache_na_v': nrm(4, (DEC_BATCH, DEPTH, PAST_LEN, NA_HEADS, NA_DH), 1.0),
        'cache_mla_ckv': nrm(5, (DEC_BATCH, DEPTH, PAST_LEN, KV_RANK), 1.0),
        'cache_mla_kpe': nrm(6, (DEC_BATCH, DEPTH, PAST_LEN, ROPE_DIM), 1.0),
        'c': nrm(7, (DEC_BATCH, D_MODEL), 1.0),
        'c_ctx': nrm(8, (D_MODEL,), 1.0),
        'w_ada': nrm(9, (DEPTH, D_MODEL, 3 * D_MODEL), 0.5 * D_MODEL ** -0.5),
        'b_ada': nrm(10, (DEPTH, 3 * D_MODEL), 0.01),
        'w_in': nrm(11, (DEPTH, D_MODEL, IN_W), D_MODEL ** -0.5),
        'b_in': nrm(12, (DEPTH, IN_W), 0.01),
        'hg_lb_logits': nrm(13, (2, DEPTH, BRANCH_W), 0.5),
        'hg_norm_g': 1.0 + nrm(14, (DEPTH, HG_DV), 0.02),
        'na_rpb': nrm(15, (DEPTH, NA_HEADS, 2 * WIN_R - 1, 2 * WIN_C - 1), 0.02),
        'mla_qnorm_g': 1.0 + nrm(16, (DEPTH, Q_RANK), 0.02),
        'mla_w_qb': nrm(17, (DEPTH, Q_RANK, MLA_HEADS * (NOPE_DIM + ROPE_DIM)), Q_RANK ** -0.5),
        'mla_kvnorm_g': 1.0 + nrm(18, (DEPTH, KV_RANK), 0.02),
        'mla_w_kvb': nrm(19, (DEPTH, KV_RANK, MLA_HEADS * (NOPE_DIM + V_DIM)), KV_RANK ** -0.5),
        'conv_w': nrm(20, (DEPTH, CONV_W, BRANCH_W), CONV_W ** -0.5),
        'w_branch': nrm(21, (DEPTH, N_BRANCH, BRANCH_W, D_MODEL), DEEPNORM_BETA * BRANCH_W ** -0.5),
        'w_out': nrm(22, (DEPTH, D_MODEL, D_MODEL), DEEPNORM_BETA * D_MODEL ** -0.5),
        'ln_g': 1.0 + nrm(23, (DEPTH, D_MODEL), 0.02),
        'ln_b': nrm(24, (DEPTH, D_MODEL), 0.01),
    }


def reference(x_prompt, x_sample, state_hgrn, cache_na_k, cache_na_v, cache_mla_ckv, cache_mla_kpe,
              c, c_ctx, w_ada, b_ada, w_in, b_in, hg_lb_logits, hg_norm_g, na_rpb, mla_qnorm_g,
              mla_w_qb, mla_kvnorm_g, mla_w_kvb, conv_w, w_branch, w_out, ln_g, ln_b):
    lb = jnp.cumsum(jax.nn.softmax(hg_lb_logits.astype(F32), axis=1), axis=1)
    lb = lb - lb[:, :1]
    y_p = x_prompt
    y_s = x_sample
    hg_s, na_k, na_v, mla_ckv, mla_kpe = [], [], [], [], []
    for l in range(DEPTH):
        weights = (w_ada[l], b_ada[l], w_in[l], b_in[l], lb[0, l], lb[1, l], hg_norm_g[l], na_rpb[l],
                   mla_qnorm_g[l], mla_w_qb[l], mla_kvnorm_g[l], mla_w_kvb[l], conv_w[l],
                   w_branch[l], w_out[l], ln_g[l], ln_b[l])
        y_p, ctx_l = trunk_layer(y_p, c_ctx, None, *weights)
        hg_s.append(ctx_l[0])
        na_k.append(ctx_l[1])
        na_v.append(ctx_l[2])
        mla_ckv.append(ctx_l[3])
        mla_kpe.append(ctx_l[4])
        cache_l = (state_hgrn[:, l], cache_na_k[:, l], cache_na_v[:, l],
                   cache_mla_ckv[:, l], cache_mla_kpe[:, l])
        y_s, _ = trunk_layer(y_s, c, cache_l, *weights)
    new_state_hgrn = jnp.stack(hg_s, axis=1)
    new_na_k = jnp.stack(na_k, axis=1)
    new_na_v = jnp.stack(na_v, axis=1)
    new_mla_ckv = jnp.stack(mla_ckv, axis=1)
    new_mla_kpe = jnp.stack(mla_kpe, axis=1)
    return (y_p, y_s, new_state_hgrn, new_na_k, new_na_v, new_mla_ckv, new_mla_kpe)
```

```python
import functools

import numpy as np
import jax
import jax.numpy as jnp
from jax import lax
from jax.experimental import pallas as pl
from jax.experimental.pallas import tpu as pltpu

F32 = jnp.float32
BF16 = jnp.bfloat16

D_MODEL = 1024
DEPTH = 2
GRID_W = 64
N_BRANCH = 4
BRANCH_W = D_MODEL // 2
HG_DK = 128
HG_DV = 128
HG_HEADS = BRANCH_W // HG_DK
HG_CHUNK = 64
HG_SUB = 16
HG_NSUB = HG_CHUNK // HG_SUB
NA_DH = 64
NA_HEADS = BRANCH_W // NA_DH
WIN_R = 8
WIN_C = 16
NOPE_DIM = 128
ROPE_DIM = 64
V_DIM = 128
MLA_HEADS = BRANCH_W // V_DIM
MLA_QW = 256
Q_RANK = 256
KV_RANK = 128
ROPE_THETA = 10000.0
CONV_W = 3
EPS = 1e-6
DEEPNORM_ALPHA = (2 * DEPTH) ** 0.25
NEG_BIG = -1e30

SPLIT_NAMES = ('a_q', 'a_ff', 'a_fb', 'a_i', 'a_g', 'b_q', 'b_k', 'b_v', 'b_g',
               'c_qd', 'c_kvd', 'c_kpe', 'c_g', 'd_b', 'd_c', 'd_x', 'd_g', 'merge')
SPLIT_SIZES = (BRANCH_W,) * 9 + (Q_RANK, KV_RANK, ROPE_DIM, BRANCH_W) + (BRANCH_W,) * 4 + (N_BRANCH * D_MODEL,)
_REF_OFF = dict(zip(SPLIT_NAMES, np.cumsum((0,) + SPLIT_SIZES[:-1]).tolist()))
_REF_SIZE = dict(zip(SPLIT_NAMES, SPLIT_SIZES))

_Z_ORDER = ('a_q', 'a_ff', 'a_fb', 'a_i', 'a_g', 'b_g', 'c_qd', 'c_kvd', 'c_kpe', 'c_kpe_sw', 'c_g',
            'd_b', 'd_c', 'd_x', 'd_g', 'b_q', 'b_k', 'b_v')


def _build_z_layout():
    off, pos = {}, 0
    for name in _Z_ORDER:
        off[name] = pos
        pos += _REF_SIZE['c_kpe' if name == 'c_kpe_sw' else name]
    return off, pos


ZOFF, ZW = _build_z_layout()


def _z_columns(w):
    parts = []
    for name in _Z_ORDER:
        if name == 'c_kpe_sw':
            base = _REF_OFF['c_kpe']
            parts += [w[..., base + ROPE_DIM // 2:base + ROPE_DIM], w[..., base:base + ROPE_DIM // 2]]
        else:
            parts.append(w[..., _REF_OFF[name]:_REF_OFF[name] + _REF_SIZE[name]])
    return jnp.concatenate(parts, axis=-1)


ZW_MAIN = ZOFF['b_q']
MERGE_OFF = _REF_OFF['merge']

LANE = 128
SUBLANE = 8
VMEM_LIMIT = 56 * 1024 * 1024


def _cparams(sem):
    return pltpu.CompilerParams(dimension_semantics=sem, vmem_limit_bytes=VMEM_LIMIT)


def _bdot(a, b):
    return jnp.dot(a.astype(BF16), b.astype(BF16), preferred_element_type=F32)


def _bdot_nt(a, b):
    return lax.dot_general(a.astype(BF16), b.astype(BF16), (((1,), (1,)), ((), ())),
                           preferred_element_type=F32)


def _bdot_tn(a, b):
    return lax.dot_general(a.astype(BF16), b.astype(BF16), (((0,), (0,)), ((), ())),
                           preferred_element_type=F32)


def _silu(x):
    return x * jax.nn.sigmoid(x)


def _mod_kernel(c_ref, w_ref, b_ref, o_ref):
    o_ref[0] = _bdot(_silu(c_ref[...]), w_ref[0]) + b_ref[0]


def _modulation(cond, w_ada, b_ada):
    n = cond.shape[0]
    tn = D_MODEL
    return pl.pallas_call(
        _mod_kernel,
        grid=(DEPTH, 3 * D_MODEL // tn),
        in_specs=[pl.BlockSpec((n, D_MODEL), lambda l, j: (0, 0)),
                  pl.BlockSpec((1, D_MODEL, tn), lambda l, j: (l, 0, j)),
                  pl.BlockSpec((1, 1, tn), lambda l, j: (l, 0, j))],
        out_specs=pl.BlockSpec((1, n, tn), lambda l, j: (l, 0, j)),
        out_shape=jax.ShapeDtypeStruct((DEPTH, n, 3 * D_MODEL), F32),
        compiler_params=_cparams(("parallel", "parallel")),
        name="adaln_mod",
    )(cond, w_ada, b_ada.reshape(DEPTH, 1, 3 * D_MODEL))


def _modulate(x, mod_row):
    shift = mod_row[:, 0:D_MODEL]
    scale = mod_row[:, D_MODEL:2 * D_MODEL]
    return x * (1.0 + scale) + shift


def _in_proj_kernel(x_ref, mod_ref, w_ref, b_ref, o_ref, h_ref):
    @pl.when(pl.program_id(1) == 0)
    def _():
        h_ref[...] = _modulate(x_ref[...], mod_ref[0]).astype(BF16)

    o_ref[...] = (jnp.dot(h_ref[...], w_ref[...], preferred_element_type=F32) + b_ref[...]).astype(o_ref.dtype)


def _in_proj(x2, mod, w, b, seq, out_dtype):
    ntok = x2.shape[0]
    n = w.shape[1]
    tm = min(1024, seq)
    tn = 512
    per_b = seq // tm
    nb = mod.shape[0]
    mod_idx = (lambda i, j: (i // per_b, 0, 0)) if nb > 1 else (lambda i, j: (0, 0, 0))
    return pl.pallas_call(
        _in_proj_kernel,
        grid=(ntok // tm, n // tn),
        in_specs=[pl.BlockSpec((tm, D_MODEL), lambda i, j: (i, 0)),
                  pl.BlockSpec((1, 1, 3 * D_MODEL), mod_idx),
                  pl.BlockSpec((D_MODEL, tn), lambda i, j: (0, j)),
                  pl.BlockSpec((1, tn), lambda i, j: (0, j))],
        out_specs=pl.BlockSpec((tm, tn), lambda i, j: (i, j)),
        out_shape=jax.ShapeDtypeStruct((ntok, n), out_dtype),
        scratch_shapes=[pltpu.VMEM((tm, D_MODEL), BF16)],
        compiler_params=_cparams(("parallel", "arbitrary")),
        name="in_proj",
    )(x2, mod, w, b)


def _hgrn_chunk(qr, fr, v, la, l1, s_t, reverse):
    L, C, NS = HG_CHUNK, HG_SUB, HG_NSUB
    q = _silu(qr)
    ls = jnp.minimum(fr, 0.0) - jnp.log1p(jnp.exp(-jnp.abs(fr)))
    c = l1 + ls
    lg = jnp.maximum(la, c) + jnp.log1p(jnp.exp(-jnp.abs(la - c)))
    kk = 1.0 - jnp.exp(lg)

    ti = lax.broadcasted_iota(jnp.int32, (L, L), 0)
    si = lax.broadcasted_iota(jnp.int32, (L, L), 1)
    tri = jnp.where((si >= ti) if reverse else (si <= ti), 1.0, 0.0).astype(F32)
    b = jnp.dot(tri, lg, precision=lax.Precision.HIGHEST, preferred_element_type=F32)

    pos = [(NS - 1 - i) if reverse else i for i in range(NS)]
    blk_at = {pos[i]: i for i in range(NS)}
    end_row = [(C * i) if reverse else (C * i + C - 1) for i in range(NS)]
    e_at = [b[end_row[blk_at[p]]:end_row[blk_at[p]] + 1, :] for p in range(NS)]
    zero_row = jnp.zeros_like(e_at[0])
    b_last = e_at[NS - 1]

    def rows(fn):
        return jnp.concatenate([jnp.broadcast_to(fn(pos[i]), (C, HG_DK)) for i in range(NS)], axis=0)

    bs = rows(lambda p: e_at[p - 1] if p > 0 else zero_row)
    be = rows(lambda p: e_at[p])
    qt = q * jnp.exp(b - bs)
    kh = kk * jnp.exp(be - b)
    qc = qt * jnp.exp(bs)
    kbar = kh * jnp.exp(b_last - be)
    q2 = qt * rows(lambda p: jnp.exp(e_at[p - 1] - e_at[p - 2]) if p >= 2 else zero_row)
    q3 = qt * rows(lambda p: jnp.exp(e_at[p - 1] - e_at[p - 3]) if p >= 3 else zero_row)

    a_all = _bdot_nt(jnp.concatenate([qt, q2, q3], axis=0), kh)
    pt = ti // C
    ps = si // C
    gap = (ps - pt) if reverse else (pt - ps)
    a_off = (jnp.where(gap == 1, a_all[0:L], 0.0) + jnp.where(gap == 2, a_all[L:2 * L], 0.0)
             + jnp.where(gap == 3, a_all[2 * L:3 * L], 0.0))
    o = _bdot(a_off, v) + _bdot_nt(qc, s_t)

    s3 = lax.broadcasted_iota(jnp.int32, (C, C, HG_DK), 0)
    t3 = lax.broadcasted_iota(jnp.int32, (C, C, HG_DK), 1)
    keep = (t3 <= s3) if reverse else (t3 >= s3)
    ones = jnp.ones((HG_DK, HG_DV), BF16)
    diag = []
    for i in range(NS):
        sl = slice(C * i, C * (i + 1))
        bb, qb, kb, vb = b[sl], q[sl], kk[sl], v[sl]
        dec = jnp.exp(jnp.where(keep, bb[None, :, :] - bb[:, None, :], NEG_BIG))
        x = (qb[None, :, :] * kb[:, None, :] * dec).reshape(C * C, HG_DK)
        rep = jnp.dot(x.astype(BF16), ones, preferred_element_type=F32)
        diag.append(jnp.sum(rep.reshape(C, C, HG_DV) * vb[:, None, :], axis=0))
    o = o + jnp.concatenate(diag, axis=0)

    s_new = s_t * jnp.exp(b_last) + _bdot_tn(v, kbar)
    return o, s_new


def _hgrn_kernel(q_ref, ff_ref, fb_ref, i_ref, lb_ref, ng_ref, s0_ref, o_ref, sT_ref, of_ref, ob_ref, *, seq):
    L = HG_CHUNK
    n = seq // L
    la_f, l1_f = lb_ref[0, 0:1, :], lb_ref[0, 1:2, :]
    la_b, l1_b = lb_ref[0, 2:3, :], lb_ref[0, 3:4, :]

    def body(i, carry):
        s_f, s_b = carry
        rf = pl.multiple_of(i * L, L)
        rb = pl.multiple_of((n - 1 - i) * L, L)
        o_f, s_f = _hgrn_chunk(q_ref[pl.ds(rf, L), :], ff_ref[pl.ds(rf, L), :], i_ref[pl.ds(rf, L), :],
                               la_f, l1_f, s_f, False)
        of_ref[pl.ds(rf, L), :] = o_f
        o_b, s_b = _hgrn_chunk(q_ref[pl.ds(rb, L), :], fb_ref[pl.ds(rb, L), :], i_ref[pl.ds(rb, L), :],
                               la_b, l1_b, s_b, True)
        ob_ref[pl.ds(rb, L), :] = o_b
        return s_f, s_b

    s_f, s_b = lax.fori_loop(0, n, body, (s0_ref[0, 0, 0], s0_ref[0, 1, 0]))
    sT_ref[0, 0, 0] = s_f
    sT_ref[0, 1, 0] = s_b

    o = of_ref[...] + ob_ref[...]
    o_ref[...] = o * lax.rsqrt(jnp.mean(o * o, axis=-1, keepdims=True) + EPS) * ng_ref[...]


def _hgrn(z, lbp, norm_g, s0_t, batch, seq):
    hb = BRANCH_W // HG_DK

    def col(name):
        base = ZOFF[name] // HG_DK
        return pl.BlockSpec((seq, HG_DK), lambda b, h: (b, base + h))

    st_spec = pl.BlockSpec((1, 2, 1, HG_DV, HG_DK), lambda b, h: (b, 0, h, 0, 0))
    return pl.pallas_call(
        functools.partial(_hgrn_kernel, seq=seq),
        grid=(batch, hb),
        in_specs=[col('a_q'), col('a_ff'), col('a_fb'), col('a_i'),
                  pl.BlockSpec((1, 4, HG_DK), lambda b, h: (h, 0, 0)),
                  pl.BlockSpec((1, HG_DV), lambda b, h: (0, 0)),
                  st_spec],
        out_specs=[pl.BlockSpec((seq, HG_DV), lambda b, h: (b, h)), st_spec],
        out_shape=[jax.ShapeDtypeStruct((batch * seq, BRANCH_W), F32),
                   jax.ShapeDtypeStruct((batch, 2, HG_HEADS, HG_DV, HG_DK), F32)],
        scratch_shapes=[pltpu.VMEM((seq, HG_DV), F32), pltpu.VMEM((seq, HG_DV), F32)],
        compiler_params=_cparams(("parallel", "parallel")),
        name="hgrn2",
    )(z, z, z, z, lbp, norm_g.reshape(1, HG_DV), s0_t)


def _flash_kernel(q_ref, k_ref, v_ref, o_ref, m_ref, l_ref, acc_ref, *, heads, dq, dv, scale):
    j = pl.program_id(2)

    @pl.when(j == 0)
    def _():
        m_ref[...] = jnp.full(m_ref.shape, NEG_BIG, F32)
        l_ref[...] = jnp.zeros(l_ref.shape, F32)
        acc_ref[...] = jnp.zeros(acc_ref.shape, F32)

    for h in range(heads):
        s = _bdot_nt(q_ref[:, h * dq:(h + 1) * dq], k_ref[:, h * dq:(h + 1) * dq])
        if scale != 1.0:
            s = s * scale
        m_prev = m_ref[h][:, 0:1]
        m_new = jnp.maximum(m_prev, jnp.max(s, axis=1, keepdims=True))
        alpha = jnp.exp(m_prev - m_new)
        p = jnp.exp(s - m_new)
        l_new = alpha * l_ref[h][:, 0:1] + jnp.sum(p, axis=1, keepdims=True)
        hs = slice(h * dv, (h + 1) * dv)
        acc_ref[:, hs] = alpha * acc_ref[:, hs] + _bdot(p, v_ref[:, hs])
        m_ref[h] = jnp.broadcast_to(m_new, m_ref.shape[1:])
        l_ref[h] = jnp.broadcast_to(l_new, l_ref.shape[1:])

    @pl.when(j == pl.num_programs(2) - 1)
    def _():
        for h in range(heads):
            hs = slice(h * dv, (h + 1) * dv)
            o_ref[:, hs] = acc_ref[:, hs] / l_ref[h][:, 0:1]


def _flash(q, k, v, batch, tq_len, tk_len, heads, dq, dv, scale, qcol=0, kcol=0, vcol=0, tq=512, tk=256):
    tq = min(tq, tq_len)
    tk = min(tk, tk_len)
    nq, nk = tq_len // tq, tk_len // tk
    return pl.pallas_call(
        functools.partial(_flash_kernel, heads=heads, dq=dq, dv=dv, scale=scale),
        grid=(batch, nq, nk),
        in_specs=[pl.BlockSpec((tq, heads * dq), lambda b, i, j: (b * nq + i, qcol)),
                  pl.BlockSpec((tk, heads * dq), lambda b, i, j: (b * nk + j, kcol)),
                  pl.BlockSpec((tk, heads * dv), lambda b, i, j: (b * nk + j, vcol))],
        out_specs=pl.BlockSpec((tq, heads * dv), lambda b, i, j: (b * nq + i, 0)),
        out_shape=jax.ShapeDtypeStruct((batch * tq_len, heads * dv), F32),
        scratch_shapes=[pltpu.VMEM((heads, tq, LANE), F32), pltpu.VMEM((heads, tq, LANE), F32),
                        pltpu.VMEM((tq, heads * dv), F32)],
        compiler_params=_cparams(("parallel", "parallel", "arbitrary")),
        name="flash_attn",
    )(q, k, v)


def _na_kernel(q_ref, k_ref, v_ref, ck_ref, cv_ref, tab_ref, o_ref, *, rows):
    r = pl.program_id(1)
    rs = jnp.clip(r - WIN_R // 2, 0, rows - WIN_R)
    k0 = pl.multiple_of(rs * GRID_W, GRID_W)
    nloc = WIN_R * GRID_W
    tab0 = rs - r + (WIN_R - 1)
    scale = NA_DH ** -0.5
    for h in range(NA_HEADS):
        hs = slice(h * NA_DH, (h + 1) * NA_DH)
        qh = q_ref[:, hs]
        bias = jnp.concatenate([tab_ref[h, tab0 + 2 * p] for p in range(WIN_R // 2)], axis=1)
        s_loc = _bdot_nt(qh, k_ref[pl.ds(k0, nloc), hs]) * scale + bias
        s_ctx = _bdot_nt(qh, ck_ref[:, hs]) * scale
        m = jnp.maximum(jnp.max(s_loc, axis=1, keepdims=True), jnp.max(s_ctx, axis=1, keepdims=True))
        p_loc = jnp.exp(s_loc - m)
        p_ctx = jnp.exp(s_ctx - m)
        denom = jnp.sum(p_loc, axis=1, keepdims=True) + jnp.sum(p_ctx, axis=1, keepdims=True)
        o = _bdot(p_loc, v_ref[pl.ds(k0, nloc), hs]) + _bdot(p_ctx, cv_ref[:, hs])
        o_ref[:, hs] = o / denom


def _na_bias_table(rpb):
    c = np.arange(GRID_W)
    cs = np.clip(c - WIN_C // 2, 0, GRID_W - WIN_C)
    kc = np.arange(GRID_W)
    valid = (kc[None, :] >= cs[:, None]) & (kc[None, :] < cs[:, None] + WIN_C)
    coff = np.clip(kc[None, :] - c[:, None] + (WIN_C - 1), 0, 2 * WIN_C - 2)
    tab = jnp.where(valid[None, None], rpb[:, :, coff], NEG_BIG)
    return jnp.concatenate([tab[:, :-1], tab[:, 1:]], axis=-1).astype(F32)


def _na_attention(zn, ck, cv, tab, batch, seq, past):
    rows = seq // GRID_W
    return pl.pallas_call(
        functools.partial(_na_kernel, rows=rows),
        grid=(batch, rows),
        in_specs=[pl.BlockSpec((GRID_W, BRANCH_W), lambda b, r: (b * rows + r, 0)),
                  pl.BlockSpec((seq, BRANCH_W), lambda b, r: (b, 1)),
                  pl.BlockSpec((seq, BRANCH_W), lambda b, r: (b, 2)),
                  pl.BlockSpec((past, BRANCH_W), lambda b, r: (b, 0)),
                  pl.BlockSpec((past, BRANCH_W), lambda b, r: (b, 0)),
                  pl.BlockSpec(tab.shape, lambda b, r: (0, 0, 0, 0))],
        out_specs=pl.BlockSpec((GRID_W, BRANCH_W), lambda b, r: (b * rows + r, 0)),
        out_shape=jax.ShapeDtypeStruct((batch * seq, BRANCH_W), F32),
        compiler_params=_cparams(("parallel", "arbitrary")),
        name="na_attn",
    )(zn, zn, zn, ck, cv, tab)


def _rms(x, g):
    return x * lax.rsqrt(jnp.mean(x * x, axis=-1, keepdims=True) + EPS) * g


def _rotate_pairs(slab, cs):
    prod = slab * cs
    both = prod + pltpu.roll(prod, ROPE_DIM, axis=1)
    lane = lax.broadcasted_iota(jnp.int32, both.shape, 1)
    return jnp.where(lane < ROPE_DIM, both, 0.0)


def _mla_prep_kernel(*refs, has_q, norm_kv):
    if has_q:
        qd_ref, kvd_ref, kpe_ref, cs_ref, qg_ref, wq_ref, kg_ref, wk_ref, wv_ref, q_ref, k_ref, v_ref, ckv_ref = refs
    else:
        kvd_ref, kpe_ref, cs_ref, kg_ref, wk_ref, wv_ref, k_ref, v_ref, ckv_ref = refs
    cs = cs_ref[...]
    if has_q:
        scale = (NOPE_DIM + ROPE_DIM) ** -0.5
        qf = _bdot(_rms(qd_ref[...], qg_ref[...]), wq_ref[...]) * scale
        for h in range(MLA_HEADS):
            base = h * MLA_QW
            q_ref[:, base:base + NOPE_DIM] = qf[:, base:base + NOPE_DIM].astype(q_ref.dtype)
            q_ref[:, base + NOPE_DIM:base + MLA_QW] = _rotate_pairs(
                qf[:, base + NOPE_DIM:base + MLA_QW], cs).astype(q_ref.dtype)
    ckv = _rms(kvd_ref[...], kg_ref[...]) if norm_kv else kvd_ref[...]
    ckv_ref[...] = ckv
    kn = _bdot(ckv, wk_ref[...])
    v_ref[...] = _bdot(ckv, wv_ref[...]).astype(v_ref.dtype)
    kr = _rotate_pairs(kpe_ref[...], cs).astype(k_ref.dtype)
    for h in range(MLA_HEADS):
        base = h * MLA_QW
        k_ref[:, base:base + NOPE_DIM] = kn[:, h * NOPE_DIM:(h + 1) * NOPE_DIM].astype(k_ref.dtype)
        k_ref[:, base + NOPE_DIM:base + MLA_QW] = kr


def _mla_prep(z, cs_tab, wts, seq, has_q, norm_kv, kvd=None, kpe=None):
    qg, wq, kg, wk, wv = wts
    ntok = z.shape[0] if z is not None else kvd.shape[0]
    tm = min(512, seq)
    per_b = seq // tm
    tok = lambda i: (i, 0)
    const = lambda i: (0, 0)
    in_specs, args = [], []
    if has_q:
        in_specs.append(pl.BlockSpec((tm, Q_RANK), lambda i: (i, ZOFF['c_qd'] // Q_RANK)))
        args.append(z)
    if z is not None:
        in_specs += [pl.BlockSpec((tm, KV_RANK), lambda i: (i, ZOFF['c_kvd'] // KV_RANK)),
                     pl.BlockSpec((tm, LANE), lambda i: (i, ZOFF['c_kpe'] // LANE))]
        args += [z, z]
    else:
        in_specs += [pl.BlockSpec((tm, KV_RANK), tok), pl.BlockSpec((tm, LANE), tok)]
        args += [kvd, kpe]
    in_specs.append(pl.BlockSpec((tm, LANE), lambda i: (i % per_b, 0)))
    args.append(cs_tab)
    if has_q:
        in_specs += [pl.BlockSpec((1, Q_RANK), const), pl.BlockSpec(wq.shape, const)]
        args += [qg.reshape(1, Q_RANK), wq]
    in_specs += [pl.BlockSpec((1, KV_RANK), const), pl.BlockSpec(wk.shape, const), pl.BlockSpec(wv.shape, const)]
    args += [kg.reshape(1, KV_RANK), wk, wv]
    out_specs = [pl.BlockSpec((tm, MLA_HEADS * MLA_QW), tok), pl.BlockSpec((tm, MLA_HEADS * V_DIM), tok),
                 pl.BlockSpec((tm, KV_RANK), tok)]
    out_shape = [jax.ShapeDtypeStruct((ntok, MLA_HEADS * MLA_QW), BF16),
                 jax.ShapeDtypeStruct((ntok, MLA_HEADS * V_DIM), BF16),
                 jax.ShapeDtypeStruct((ntok, KV_RANK), F32)]
    if has_q:
        out_specs.insert(0, pl.BlockSpec((tm, MLA_HEADS * MLA_QW), tok))
        out_shape.insert(0, jax.ShapeDtypeStruct((ntok, MLA_HEADS * MLA_QW), BF16))
    return pl.pallas_call(
        functools.partial(_mla_prep_kernel, has_q=has_q, norm_kv=norm_kv),
        grid=(ntok // tm,),
        in_specs=in_specs, out_specs=out_specs, out_shape=out_shape,
        compiler_params=_cparams(("parallel",)),
        name="mla_prep",
    )(*args)


def _rope_table(seq):
    t = jnp.arange(seq)
    row = (t // GRID_W).astype(F32)
    col = (t % GRID_W).astype(F32)
    n_pair_axis = ROPE_DIM // 4
    inv = 1.0 / (ROPE_THETA ** (jnp.arange(n_pair_axis, dtype=F32) / n_pair_axis))
    ang = jnp.concatenate([row[:, None] * inv, col[:, None] * inv], axis=-1)
    cos, sin = jnp.cos(ang), jnp.sin(ang)
    return jnp.concatenate([cos, cos, -sin, sin], axis=-1)


def _identity_table(seq):
    return jnp.concatenate([jnp.ones((seq, ROPE_DIM), F32), jnp.zeros((seq, ROPE_DIM), F32)], axis=-1)


def _merge_kernel(x_ref, mod_ref, oa_ref, ob_ref, oc_ref, ga_ref, gb_ref, gc_ref, db_ref, dc_ref, dx_ref, dg_ref,
                  pc_ref, px_ref, nc_ref, nx_ref, wm_ref, bm_ref, wb_ref, wo_ref, cw_ref, lg_ref, lb_ref, y_ref,
                  *, per_b):
    tm = x_ref.shape[0]
    ti = pl.program_id(0) % per_b
    x = x_ref[...]
    mod = mod_ref[0]
    h = _modulate(x, mod).astype(BF16)
    gate = mod[:, 2 * D_MODEL:3 * D_MODEL]

    u = dc_ref[...] * dx_ref[...]
    prev_row = jnp.where(ti > 0, pc_ref[SUBLANE - 1:SUBLANE, :] * px_ref[SUBLANE - 1:SUBLANE, :], 0.0)
    next_row = jnp.where(ti < per_b - 1, nc_ref[0:1, :] * nx_ref[0:1, :], 0.0)
    row = lax.broadcasted_iota(jnp.int32, u.shape, 0)
    u_prev = jnp.where(row == 0, prev_row, pltpu.roll(u, 1, axis=0))
    u_next = jnp.where(row == tm - 1, next_row, pltpu.roll(u, tm - 1, axis=0))
    conv = cw_ref[0:1, :] * u_prev + cw_ref[1:2, :] * u + cw_ref[2:3, :] * u_next
    out_d = db_ref[...] * conv * _silu(dg_ref[...])

    branches = (oa_ref[...] * _silu(ga_ref[...]), ob_ref[...] * _silu(gb_ref[...]),
                oc_ref[...] * _silu(gc_ref[...]), out_d)
    mixed = jnp.zeros((tm, D_MODEL), F32)
    for n in range(N_BRANCH):
        cols = slice(n * D_MODEL, (n + 1) * D_MODEL)
        mg = jnp.dot(h, wm_ref[:, cols], preferred_element_type=F32) + bm_ref[:, cols]
        mixed = mixed + jax.nn.sigmoid(mg) * _bdot(branches[n], wb_ref[n])
    out = _bdot(mixed, wo_ref[...])
    r = DEEPNORM_ALPHA * x + gate * out
    mu = jnp.mean(r, axis=-1, keepdims=True)
    rc = r - mu
    var = jnp.mean(rc * rc, axis=-1, keepdims=True)
    y_ref[...] = rc * lax.rsqrt(var + EPS) * lg_ref[...] + lb_ref[...]


def _merge(x2, mod, z, o_a, o_b, o_c, wts, seq):
    wm, bm, wb, wo, cw, ln_g, ln_b = wts
    ntok = x2.shape[0]
    tm = min(256, seq)
    per_b = seq // tm
    nb = mod.shape[0]
    hb = tm // SUBLANE
    last_hb = ntok // SUBLANE - 1
    mod_idx = (lambda i: (i // per_b, 0, 0)) if nb > 1 else (lambda i: (0, 0, 0))
    tok = lambda i: (i, 0)
    const2 = lambda i: (0, 0)

    def zcol(name):
        blk = ZOFF[name] // BRANCH_W
        return pl.BlockSpec((tm, BRANCH_W), lambda i: (i, blk))

    def halo(name, nxt):
        blk = ZOFF[name] // BRANCH_W
        if nxt:
            return pl.BlockSpec((SUBLANE, BRANCH_W), lambda i: (jnp.minimum((i + 1) * hb, last_hb), blk))
        return pl.BlockSpec((SUBLANE, BRANCH_W), lambda i: (jnp.maximum(i * hb - 1, 0), blk))

    br = pl.BlockSpec((tm, BRANCH_W), tok)
    in_specs = [pl.BlockSpec((tm, D_MODEL), tok), pl.BlockSpec((1, 1, 3 * D_MODEL), mod_idx),
                br, br, br, zcol('a_g'), zcol('b_g'), zcol('c_g'),
                zcol('d_b'), zcol('d_c'), zcol('d_x'), zcol('d_g'),
                halo('d_c', False), halo('d_x', False), halo('d_c', True), halo('d_x', True),
                pl.BlockSpec(wm.shape, const2), pl.BlockSpec(bm.shape, const2),
                pl.BlockSpec(wb.shape, lambda i: (0, 0, 0)), pl.BlockSpec(wo.shape, const2),
                pl.BlockSpec(cw.shape, const2), pl.BlockSpec((1, D_MODEL), const2),
                pl.BlockSpec((1, D_MODEL), const2)]
    return pl.pallas_call(
        functools.partial(_merge_kernel, per_b=per_b),
        grid=(ntok // tm,),
        in_specs=in_specs,
        out_specs=pl.BlockSpec((tm, D_MODEL), tok),
        out_shape=jax.ShapeDtypeStruct((ntok, D_MODEL), F32),
        compiler_params=_cparams(("parallel",)),
        name="merge_out",
    )(x2, mod, o_a, o_b, o_c, z, z, z, z, z, z, z, z, z, z, z, wm, bm, wb, wo, cw,
      ln_g.reshape(1, D_MODEL), ln_b.reshape(1, D_MODEL))


def _layer_weights(l, w_in, b_in, lb, hg_norm_g, na_rpb, mla_qnorm_g, mla_w_qb, mla_kvnorm_g, mla_w_kvb,
                   conv_w, w_branch, w_out, ln_g, ln_b):
    w_in_l = w_in[l]
    wz = _z_columns(w_in_l).astype(BF16)
    bz = _z_columns(b_in[l]).reshape(1, ZW)
    wm = w_in_l[:, MERGE_OFF:].astype(BF16)
    bm = b_in[l, MERGE_OFF:].reshape(1, N_BRANCH * D_MODEL)
    lb_f, lb_b = lb[0, l], lb[1, l]
    lbp = jnp.stack([jnp.log(lb_f), jnp.log1p(-lb_f), jnp.log(lb_b), jnp.log1p(-lb_b)], axis=0)
    lbp = jnp.transpose(lbp.reshape(4, HG_HEADS, HG_DK), (1, 0, 2))
    wq3 = mla_w_qb[l].reshape(Q_RANK, MLA_HEADS, NOPE_DIM + ROPE_DIM)
    half = NOPE_DIM + ROPE_DIM // 2
    wq = jnp.concatenate([wq3, wq3[..., half:], wq3[..., NOPE_DIM:half]], axis=-1)
    wq = wq.reshape(Q_RANK, MLA_HEADS * MLA_QW).astype(BF16)
    wkv3 = mla_w_kvb[l].reshape(KV_RANK, MLA_HEADS, NOPE_DIM + V_DIM)
    wk = wkv3[..., :NOPE_DIM].reshape(KV_RANK, MLA_HEADS * NOPE_DIM).astype(BF16)
    wv = wkv3[..., NOPE_DIM:].reshape(KV_RANK, MLA_HEADS * V_DIM).astype(BF16)
    return dict(
        wz=wz, bz=bz, lbp=lbp, hg_norm_g=hg_norm_g[l], na_tab=_na_bias_table(na_rpb[l]),
        mla=(mla_qnorm_g[l], wq, mla_kvnorm_g[l], wk, wv),
        merge=(wm, bm, w_branch[l].astype(BF16), w_out[l].astype(BF16), conv_w[l], ln_g[l], ln_b[l]))


def _context_layer(x2, mod, w, batch, seq):
    z = _in_proj(x2, mod, w['wz'], w['bz'], seq, F32)
    s0 = jnp.zeros((batch, 2, HG_HEADS, HG_DV, HG_DK), F32)
    o_a, s_t = _hgrn(z, w['lbp'], w['hg_norm_g'], s0, batch, seq)
    cq, ck, cv = (ZOFF[n] // BRANCH_W for n in ('b_q', 'b_k', 'b_v'))
    o_b = _flash(z, z, z, batch, seq, seq, NA_HEADS, NA_DH, NA_DH, NA_DH ** -0.5, cq, ck, cv)
    q, k, v, ckv = _mla_prep(z, _identity_table(seq), w['mla'], seq, True, True)
    o_c = _flash(q, k, v, batch, seq, seq, MLA_HEADS, MLA_QW, V_DIM, 1.0)
    y = _merge(x2, mod, z, o_a, o_b, o_c, w['merge'], seq)

    def zslice(name, width):
        return z[:, ZOFF[name]:ZOFF[name] + width]

    cache = (jnp.swapaxes(s_t, -1, -2),
             zslice('b_k', BRANCH_W).reshape(batch, seq, NA_HEADS, NA_DH),
             zslice('b_v', BRANCH_W).reshape(batch, seq, NA_HEADS, NA_DH),
             ckv.reshape(batch, seq, KV_RANK),
             zslice('c_kpe', ROPE_DIM).reshape(batch, seq, ROPE_DIM))
    return y, cache


def _latent_layer(x2, mod, w, cache, batch, seq):
    s0, na_k, na_v, c_ckv, c_kpe = cache
    past = na_k.shape[1]
    z = _in_proj(x2, mod, w['wz'][:, :ZW_MAIN], w['bz'][:, :ZW_MAIN], seq, F32)
    zn = _in_proj(x2, mod, w['wz'][:, ZW_MAIN:], w['bz'][:, ZW_MAIN:], seq, BF16)
    o_a, _ = _hgrn(z, w['lbp'], w['hg_norm_g'], jnp.swapaxes(s0, -1, -2), batch, seq)
    o_b = _na_attention(zn, na_k.reshape(batch * past, BRANCH_W).astype(BF16),
                        na_v.reshape(batch * past, BRANCH_W).astype(BF16), w['na_tab'], batch, seq, past)
    q, k, v, _ = _mla_prep(z, _rope_table(seq), w['mla'], seq, True, True)
    kpe_pad = jnp.pad(c_kpe.reshape(batch * past, ROPE_DIM), ((0, 0), (0, LANE - ROPE_DIM)))
    k_c, v_c, _ = _mla_prep(None, _identity_table(past), w['mla'], past, False, False,
                            kvd=c_ckv.reshape(batch * past, KV_RANK), kpe=kpe_pad)
    k_all = jnp.concatenate([k.reshape(batch, seq, -1), k_c.reshape(batch, past, -1)], axis=1)
    v_all = jnp.concatenate([v.reshape(batch, seq, -1), v_c.reshape(batch, past, -1)], axis=1)
    tk_len = seq + past
    o_c = _flash(q, k_all.reshape(batch * tk_len, -1), v_all.reshape(batch * tk_len, -1),
                 batch, seq, tk_len, MLA_HEADS, MLA_QW, V_DIM, 1.0)
    return _merge(x2, mod, z, o_a, o_b, o_c, w['merge'], seq)


def kernel(x_prompt, x_sample, state_hgrn, cache_na_k, cache_na_v, cache_mla_ckv, cache_mla_kpe, c, c_ctx,
           w_ada, b_ada, w_in, b_in, hg_lb_logits, hg_norm_g, na_rpb, mla_qnorm_g, mla_w_qb, mla_kvnorm_g,
           mla_w_kvb, conv_w, w_branch, w_out, ln_g, ln_b):
    batch, seq, _ = x_prompt.shape
    dbatch, dseq, _ = x_sample.shape
    lb = jnp.cumsum(jax.nn.softmax(hg_lb_logits.astype(F32), axis=1), axis=1)
    lb = lb - lb[:, :1]
    n_cond = -(-(dbatch + 1) // SUBLANE) * SUBLANE
    cond = jnp.zeros((n_cond, D_MODEL), F32).at[:dbatch].set(c).at[dbatch].set(c_ctx)
    mod = _modulation(cond, w_ada, b_ada)

    y_p = x_prompt.reshape(batch * seq, D_MODEL)
    y_s = x_sample.reshape(dbatch * dseq, D_MODEL)
    caches = []
    for l in range(DEPTH):
        w = _layer_weights(l, w_in, b_in, lb, hg_norm_g, na_rpb, mla_qnorm_g, mla_w_qb, mla_kvnorm_g,
                           mla_w_kvb, conv_w, w_branch, w_out, ln_g, ln_b)
        mod_ctx = mod[l, dbatch:dbatch + 1].reshape(1, 1, 3 * D_MODEL)
        mod_lat = mod[l, :dbatch].reshape(dbatch, 1, 3 * D_MODEL)
        y_p, cache_l = _context_layer(y_p, mod_ctx, w, batch, seq)
        caches.append(cache_l)
        y_s = _latent_layer(y_s, mod_lat, w,
                            (state_hgrn[:, l], cache_na_k[:, l], cache_na_v[:, l],
                             cache_mla_ckv[:, l], cache_mla_kpe[:, l]), dbatch, dseq)
    outs = [jnp.stack([cl[i] for cl in caches], axis=1) for i in range(5)]
    return (y_p.reshape(batch, seq, D_MODEL), y_s.reshape(dbatch, dseq, D_MODEL), *outs)
```

```python
import functools

import numpy as np
import jax
import jax.numpy as jnp
from jax import lax
from jax.experimental import pallas as pl
from jax.experimental.pallas import tpu as pltpu

F32 = jnp.float32
BF16 = jnp.bfloat16

D_MODEL = 1024
DEPTH = 2
GRID_W = 64
N_BRANCH = 4
BRANCH_W = D_MODEL // 2
HG_DK = 128
HG_DV = 128
HG_HEADS = BRANCH_W // HG_DK
HG_CHUNK = 64
HG_SUB = 16
HG_NSUB = HG_CHUNK // HG_SUB
HG_SAFE_DECAY = 60.0
HG_PRE_ROWS = 512
HG_UNROLL = 4
NA_DH = 64
NA_HEADS = BRANCH_W // NA_DH
WIN_R = 8
WIN_C = 16
NOPE_DIM = 128
ROPE_DIM = 64
V_DIM = 128
MLA_HEADS = BRANCH_W // V_DIM
MLA_QW = 256
MLA_VW = 256
MLA_KV_CHUNK = 1024
Q_RANK = 256
KV_RANK = 128
ROPE_THETA = 10000.0
CONV_W = 3
EPS = 1e-6
DEEPNORM_ALPHA = (2 * DEPTH) ** 0.25
NEG_BIG = -1e30

SPLIT_NAMES = ('a_q', 'a_ff', 'a_fb', 'a_i', 'a_g', 'b_q', 'b_k', 'b_v', 'b_g',
               'c_qd', 'c_kvd', 'c_kpe', 'c_g', 'd_b', 'd_c', 'd_x', 'd_g', 'merge')
SPLIT_SIZES = (BRANCH_W,) * 9 + (Q_RANK, KV_RANK, ROPE_DIM, BRANCH_W) + (BRANCH_W,) * 4 + (N_BRANCH * D_MODEL,)
_REF_OFF = dict(zip(SPLIT_NAMES, np.cumsum((0,) + SPLIT_SIZES[:-1]).tolist()))
_REF_SIZE = dict(zip(SPLIT_NAMES, SPLIT_SIZES))

_Z_ORDER = ('a_q', 'a_ff', 'a_fb', 'a_i', 'a_g', 'b_g', 'c_qd', 'c_kvd', 'c_kpe', 'c_kpe_sw', 'c_g',
            'd_b', 'd_c', 'd_x', 'd_g', 'b_q', 'b_k', 'b_v')


def _build_z_layout():
    off, pos = {}, 0
    for name in _Z_ORDER:
        off[name] = pos
        pos += _REF_SIZE['c_kpe' if name == 'c_kpe_sw' else name]
    return off, pos


ZOFF, ZW = _build_z_layout()


def _z_columns(w):
    parts = []
    for name in _Z_ORDER:
        if name == 'c_kpe_sw':
            base = _REF_OFF['c_kpe']
            parts += [w[..., base + ROPE_DIM // 2:base + ROPE_DIM], w[..., base:base + ROPE_DIM // 2]]
        else:
            parts.append(w[..., _REF_OFF[name]:_REF_OFF[name] + _REF_SIZE[name]])
    return jnp.concatenate(parts, axis=-1)


ZW_MAIN = ZOFF['b_q']
MERGE_OFF = _REF_OFF['merge']

LANE = 128
SUBLANE = 8
VMEM_LIMIT = 56 * 1024 * 1024


def _cparams(sem):
    return pltpu.CompilerParams(dimension_semantics=sem, vmem_limit_bytes=VMEM_LIMIT)


def _bdot(a, b):
    return jnp.dot(a.astype(BF16), b.astype(BF16), preferred_element_type=F32)


def _bdot_nt(a, b):
    return lax.dot_general(a.astype(BF16), b.astype(BF16), (((1,), (1,)), ((), ())),
                           preferred_element_type=F32)


def _bdot_tn(a, b):
    return lax.dot_general(a.astype(BF16), b.astype(BF16), (((0,), (0,)), ((), ())),
                           preferred_element_type=F32)


def _silu(x):
    return x * jax.nn.sigmoid(x)


def _mod_kernel(c_ref, w_ref, b_ref, o_ref):
    o_ref[0] = _bdot(_silu(c_ref[...]), w_ref[0]) + b_ref[0]


def _modulation(cond, w_ada, b_ada):
    n = cond.shape[0]
    tn = D_MODEL
    return pl.pallas_call(
        _mod_kernel,
        grid=(DEPTH, 3 * D_MODEL // tn),
        in_specs=[pl.BlockSpec((n, D_MODEL), lambda l, j: (0, 0)),
                  pl.BlockSpec((1, D_MODEL, tn), lambda l, j: (l, 0, j)),
                  pl.BlockSpec((1, 1, tn), lambda l, j: (l, 0, j))],
        out_specs=pl.BlockSpec((1, n, tn), lambda l, j: (l, 0, j)),
        out_shape=jax.ShapeDtypeStruct((DEPTH, n, 3 * D_MODEL), F32),
        compiler_params=_cparams(("parallel", "parallel")),
        name="adaln_mod",
    )(cond, w_ada, b_ada.reshape(DEPTH, 1, 3 * D_MODEL))


def _modulate(x, mod_row):
    shift = mod_row[:, 0:D_MODEL]
    scale = mod_row[:, D_MODEL:2 * D_MODEL]
    return x * (1.0 + scale) + shift


def _in_proj_kernel(x_ref, mod_ref, w_ref, b_ref, o_ref, h_ref):
    @pl.when(pl.program_id(1) == 0)
    def _():
        h_ref[...] = _modulate(x_ref[...], mod_ref[0]).astype(BF16)

    o_ref[...] = (jnp.dot(h_ref[...], w_ref[...], preferred_element_type=F32) + b_ref[...]).astype(o_ref.dtype)


def _in_proj(x2, mod, w, b, seq, out_dtype):
    ntok = x2.shape[0]
    n = w.shape[1]
    tm = min(1024, seq)
    tn = 512
    per_b = seq // tm
    nb = mod.shape[0]
    mod_idx = (lambda i, j: (i // per_b, 0, 0)) if nb > 1 else (lambda i, j: (0, 0, 0))
    return pl.pallas_call(
        _in_proj_kernel,
        grid=(ntok // tm, n // tn),
        in_specs=[pl.BlockSpec((tm, D_MODEL), lambda i, j: (i, 0)),
                  pl.BlockSpec((1, 1, 3 * D_MODEL), mod_idx),
                  pl.BlockSpec((D_MODEL, tn), lambda i, j: (0, j)),
                  pl.BlockSpec((1, tn), lambda i, j: (0, j))],
        out_specs=pl.BlockSpec((tm, tn), lambda i, j: (i, j)),
        out_shape=jax.ShapeDtypeStruct((ntok, n), out_dtype),
        scratch_shapes=[pltpu.VMEM((tm, D_MODEL), BF16)],
        compiler_params=_cparams(("parallel", "arbitrary")),
        name="in_proj",
    )(x2, mod, w, b)


def _hgrn_log_gate(fr, la, l1):
    ls = jnp.minimum(fr, 0.0) - jnp.log(1.0 + jnp.exp(-jnp.abs(fr)))
    c = l1 + ls
    return jnp.maximum(la, c) + jnp.log(1.0 + jnp.exp(-jnp.abs(la - c)))


def _chunk_cumsum(x, row_in_chunk, reverse):
    n = x.shape[0]
    k = 1
    while k < HG_CHUNK:
        if reverse:
            x = x + jnp.where(row_in_chunk < HG_CHUNK - k, pltpu.roll(x, n - k, axis=0), 0.0)
        else:
            x = x + jnp.where(row_in_chunk >= k, pltpu.roll(x, k, axis=0), 0.0)
        k *= 2
    return x


def _hgrn_chunk_factors(q, kk, b, reverse, factorised):
    C, NS = HG_SUB, HG_NSUB

    pos = [(NS - 1 - i) if reverse else i for i in range(NS)]
    blk_at = {pos[i]: i for i in range(NS)}
    end_row = [(C * i) if reverse else (C * i + C - 1) for i in range(NS)]
    e_at = [b[end_row[blk_at[p]]:end_row[blk_at[p]] + 1, :] for p in range(NS)]
    zero_row = jnp.zeros_like(e_at[0])
    b_last = e_at[NS - 1]

    def rows(fn):
        return jnp.concatenate([jnp.broadcast_to(fn(pos[i]), (C, HG_DK)) for i in range(NS)], axis=0)

    bs = rows(lambda p: e_at[p - 1] if p > 0 else zero_row)
    be = rows(lambda p: e_at[p])
    qt = q * jnp.exp(b - bs)
    kh = kk * jnp.exp(be - b)
    qc = qt * jnp.exp(bs)
    kbar = kh * jnp.exp(b_last - be)
    q2 = qt * rows(lambda p: jnp.exp(e_at[p - 1] - e_at[p - 2]) if p >= 2 else zero_row)
    q3 = qt * rows(lambda p: jnp.exp(e_at[p - 1] - e_at[p - 3]) if p >= 3 else zero_row)
    kd = kk * jnp.exp(bs - b) if factorised else None
    return dict(qs=jnp.concatenate([qt, q2, q3], axis=0).astype(BF16), qt=qt.astype(BF16), kh=kh.astype(BF16),
                kd=None if kd is None else kd.astype(BF16), qc=qc.astype(BF16), kbar=kbar.astype(BF16),
                state_decay=jnp.exp(b_last))


def _hgrn_chunk_scores(f, reverse):
    L, C = HG_CHUNK, HG_SUB
    ti = lax.broadcasted_iota(jnp.int32, (L, L), 0)
    si = lax.broadcasted_iota(jnp.int32, (L, L), 1)
    a_all = _bdot_nt(f['qs'], f['kh'])
    pt = ti // C
    ps = si // C
    gap = (ps - pt) if reverse else (pt - ps)
    a = (jnp.where(gap == 1, a_all[0:L], 0.0) + jnp.where(gap == 2, a_all[L:2 * L], 0.0)
         + jnp.where(gap == 3, a_all[2 * L:3 * L], 0.0))
    if f['kd'] is not None:
        before = (si >= ti) if reverse else (si <= ti)
        a = a + jnp.where((gap == 0) & before, _bdot_nt(f['qt'], f['kd']), 0.0)
    return a.astype(BF16)


def _hgrn_pairwise_diag(q, kk, b, v, reverse):
    C, NS = HG_SUB, HG_NSUB
    s3 = lax.broadcasted_iota(jnp.int32, (C, C, HG_DK), 0)
    t3 = lax.broadcasted_iota(jnp.int32, (C, C, HG_DK), 1)
    keep = (t3 <= s3) if reverse else (t3 >= s3)
    ones = jnp.ones((HG_DK, HG_DV), BF16)
    diag = []
    for i in range(NS):
        sl = slice(C * i, C * (i + 1))
        bb, qb, kb, vb = b[sl], q[sl], kk[sl], v[sl]
        dec = jnp.exp(jnp.where(keep, bb[None, :, :] - bb[:, None, :], NEG_BIG))
        x = (qb[None, :, :] * kb[:, None, :] * dec).reshape(C * C, HG_DK)
        rep = jnp.dot(x.astype(BF16), ones, preferred_element_type=F32)
        diag.append(jnp.sum(rep.reshape(C, C, HG_DV) * vb[:, None, :], axis=0))
    return jnp.concatenate(diag, axis=0)


def _hgrn_chunks(ins, states, factorised):
    fac = [_hgrn_chunk_factors(q, kk, b, rev, factorised) for q, kk, b, v, rev, _ in ins]
    att = [_hgrn_chunk_scores(f, c[4]) for f, c in zip(fac, ins)]
    intra = [_bdot(a, c[3]) for a, c in zip(att, ins)]
    upd = [_bdot_tn(c[3], f['kbar']) for f, c in zip(fac, ins)]
    if not factorised:
        intra = [o + _hgrn_pairwise_diag(q, kk, b, v, rev) for o, (q, kk, b, v, rev, _) in zip(intra, ins)]
    states = list(states)
    outs = []
    for f, c, o, u in zip(fac, ins, intra, upd):
        s_t = states[c[5]]
        outs.append(o + _bdot_nt(f['qc'], s_t))
        states[c[5]] = s_t * f['state_decay'] + u
    return outs, states


def _hgrn_kernel(q_ref, ff_ref, fb_ref, i_ref, lb_ref, ng_ref, s0_ref, o_ref, sT_ref,
                 qs_ref, kf_ref, kb_ref, bf_ref, bb_ref, of_ref, ob_ref, *, seq):
    L, C = HG_CHUNK, HG_SUB
    n = seq // L
    la_f, l1_f = lb_ref[0, 0:1, :], lb_ref[0, 1:2, :]
    la_b, l1_b = lb_ref[0, 2:3, :], lb_ref[0, 3:4, :]

    pb = min(HG_PRE_ROWS, seq)
    row_in_chunk = lax.broadcasted_iota(jnp.int32, (pb, HG_DK), 0) % L

    def pre(i, lo):
        r = pl.ds(pl.multiple_of(i * pb, pb), pb)
        qs_ref[r, :] = _silu(q_ref[r, :])
        lgf = _hgrn_log_gate(ff_ref[r, :], la_f, l1_f)
        lgb = _hgrn_log_gate(fb_ref[r, :], la_b, l1_b)
        kf_ref[r, :] = 1.0 - jnp.exp(lgf)
        kb_ref[r, :] = 1.0 - jnp.exp(lgb)
        bf_ref[r, :] = _chunk_cumsum(lgf, row_in_chunk, False)
        bb_ref[r, :] = _chunk_cumsum(lgb, row_in_chunk, True)
        tot = jnp.minimum(jnp.sum(lgf.reshape(pb // C, C, HG_DK), axis=1),
                          jnp.sum(lgb.reshape(pb // C, C, HG_DK), axis=1))
        return jnp.minimum(lo, jnp.min(tot, axis=0, keepdims=True))

    lo = lax.fori_loop(0, seq // pb, pre, jnp.zeros((1, HG_DK), F32))
    bounded = jnp.min(lo) > -HG_SAFE_DECAY

    def scan(factorised):
        u = min(HG_UNROLL, n) if factorised else 1

        def body(g, carry):
            s_f, s_b = carry
            rows_f = [pl.ds(pl.multiple_of((g * u + j) * L, L), L) for j in range(u)]
            rows_b = [pl.ds(pl.multiple_of((n - 1 - g * u - j) * L, L), L) for j in range(u)]
            ins = []
            for j in range(u):
                ins.append((qs_ref[rows_f[j], :], kf_ref[rows_f[j], :], bf_ref[rows_f[j], :],
                            i_ref[rows_f[j], :], False, 0))
                ins.append((qs_ref[rows_b[j], :], kb_ref[rows_b[j], :], bb_ref[rows_b[j], :],
                            i_ref[rows_b[j], :], True, 1))
            outs, (s_f, s_b) = _hgrn_chunks(ins, (s_f, s_b), factorised)
            for j in range(u):
                of_ref[rows_f[j], :] = outs[2 * j]
                ob_ref[rows_b[j], :] = outs[2 * j + 1]
            return s_f, s_b

        s_f, s_b = lax.fori_loop(0, n // u, body, (s0_ref[0, 0, 0], s0_ref[0, 1, 0]))
        sT_ref[0, 0, 0] = s_f
        sT_ref[0, 1, 0] = s_b

    pl.when(bounded)(lambda: scan(True))
    pl.when(jnp.logical_not(bounded))(lambda: scan(False))

    o = of_ref[...] + ob_ref[...]
    o_ref[...] = o * lax.rsqrt(jnp.mean(o * o, axis=-1, keepdims=True) + EPS) * ng_ref[...]


def _hgrn(z, lbp, norm_g, s0_t, batch, seq):
    hb = BRANCH_W // HG_DK

    def col(name):
        base = ZOFF[name] // HG_DK
        return pl.BlockSpec((seq, HG_DK), lambda b, h: (b, base + h))

    st_spec = pl.BlockSpec((1, 2, 1, HG_DV, HG_DK), lambda b, h: (b, 0, h, 0, 0))
    return pl.pallas_call(
        functools.partial(_hgrn_kernel, seq=seq),
        grid=(batch, hb),
        in_specs=[col('a_q'), col('a_ff'), col('a_fb'), col('a_i'),
                  pl.BlockSpec((1, 4, HG_DK), lambda b, h: (h, 0, 0)),
                  pl.BlockSpec((1, HG_DV), lambda b, h: (0, 0)),
                  st_spec],
        out_specs=[pl.BlockSpec((seq, HG_DV), lambda b, h: (b, h)), st_spec],
        out_shape=[jax.ShapeDtypeStruct((batch * seq, BRANCH_W), F32),
                   jax.ShapeDtypeStruct((batch, 2, HG_HEADS, HG_DV, HG_DK), F32)],
        scratch_shapes=[pltpu.VMEM((seq, HG_DK), F32)] * 5 + [pltpu.VMEM((seq, HG_DV), F32)] * 2,
        compiler_params=_cparams(("parallel", "parallel")),
        name="hgrn2",
    )(z, z, z, z, lbp, norm_g.reshape(1, HG_DV), s0_t)


def _flash_kernel(q_ref, k_ref, v_ref, o_ref, m_ref, l_ref, acc_ref, *, heads, dq, dv, scale):
    j = pl.program_id(2)

    @pl.when(j == 0)
    def _():
        m_ref[...] = jnp.full(m_ref.shape, NEG_BIG, F32)
        l_ref[...] = jnp.zeros(l_ref.shape, F32)
        acc_ref[...] = jnp.zeros(acc_ref.shape, F32)

    for h in range(heads):
        s = _bdot_nt(q_ref[:, h * dq:(h + 1) * dq], k_ref[:, h * dq:(h + 1) * dq])
        if scale != 1.0:
            s = s * scale
        m_prev = m_ref[h][:, 0:1]
        m_new = jnp.maximum(m_prev, jnp.max(s, axis=1, keepdims=True))
        alpha = jnp.exp(m_prev - m_new)
        p = jnp.exp(s - m_new)
        l_new = alpha * l_ref[h][:, 0:1] + jnp.sum(p, axis=1, keepdims=True)
        hs = slice(h * dv, (h + 1) * dv)
        acc_ref[:, hs] = alpha * acc_ref[:, hs] + _bdot(p, v_ref[:, hs])
        m_ref[h] = jnp.broadcast_to(m_new, m_ref.shape[1:])
        l_ref[h] = jnp.broadcast_to(l_new, l_ref.shape[1:])

    @pl.when(j == pl.num_programs(2) - 1)
    def _():
        for h in range(heads):
            hs = slice(h * dv, (h + 1) * dv)
            o_ref[:, hs] = acc_ref[:, hs] / l_ref[h][:, 0:1]


def _flash(q, k, v, batch, tq_len, tk_len, heads, dq, dv, scale, qcol=0, kcol=0, vcol=0, tq=512, tk=256):
    tq = min(tq, tq_len)
    tk = min(tk, tk_len)
    nq, nk = tq_len // tq, tk_len // tk
    return pl.pallas_call(
        functools.partial(_flash_kernel, heads=heads, dq=dq, dv=dv, scale=scale),
        grid=(batch, nq, nk),
        in_specs=[pl.BlockSpec((tq, heads * dq), lambda b, i, j: (b * nq + i, qcol)),
                  pl.BlockSpec((tk, heads * dq), lambda b, i, j: (b * nk + j, kcol)),
                  pl.BlockSpec((tk, heads * dv), lambda b, i, j: (b * nk + j, vcol))],
        out_specs=pl.BlockSpec((tq, heads * dv), lambda b, i, j: (b * nq + i, 0)),
        out_shape=jax.ShapeDtypeStruct((batch * tq_len, heads * dv), F32),
        scratch_shapes=[pltpu.VMEM((heads, tq, LANE), F32), pltpu.VMEM((heads, tq, LANE), F32),
                        pltpu.VMEM((tq, heads * dv), F32)],
        compiler_params=_cparams(("parallel", "parallel", "arbitrary")),
        name="flash_attn",
    )(q, k, v)


def _mla_attn_kernel(q_ref, k_ref, v_ref, o_ref, *, chunks):
    tq = q_ref.shape[0]
    for h in range(MLA_HEADS):
        qs = slice(h * MLA_QW, (h + 1) * MLA_QW)
        vs = slice(h * MLA_VW, (h + 1) * MLA_VW)
        qh = q_ref[:, qs]
        m = jnp.full((tq, 1), NEG_BIG, F32)
        acc = jnp.zeros((tq, MLA_VW), F32)
        for c0, cn in chunks:
            s = _bdot_nt(qh, k_ref[c0:c0 + cn, qs])
            m_new = jnp.maximum(m, jnp.max(s, axis=1, keepdims=True))
            p = jnp.exp(s - m_new)
            acc = jnp.exp(m - m_new) * acc + _bdot(p, v_ref[c0:c0 + cn, vs])
            m = m_new
        o_ref[:, h * V_DIM:(h + 1) * V_DIM] = acc[:, :V_DIM] / acc[:, V_DIM:2 * V_DIM]


def _mla_attention(q, k, v, batch, tq_len, tk_len, tq=512):
    tq = min(tq, tq_len)
    nq = tq_len // tq
    chunks, c0 = [], 0
    while c0 < tk_len:
        cn = min(MLA_KV_CHUNK, tk_len - c0)
        chunks.append((c0, cn))
        c0 += cn
    return pl.pallas_call(
        functools.partial(_mla_attn_kernel, chunks=tuple(chunks)),
        grid=(batch, nq),
        in_specs=[pl.BlockSpec((tq, MLA_HEADS * MLA_QW), lambda b, i: (b * nq + i, 0)),
                  pl.BlockSpec((tk_len, MLA_HEADS * MLA_QW), lambda b, i: (b, 0)),
                  pl.BlockSpec((tk_len, MLA_HEADS * MLA_VW), lambda b, i: (b, 0))],
        out_specs=pl.BlockSpec((tq, MLA_HEADS * V_DIM), lambda b, i: (b * nq + i, 0)),
        out_shape=jax.ShapeDtypeStruct((batch * tq_len, MLA_HEADS * V_DIM), F32),
        compiler_params=_cparams(("parallel", "arbitrary")),
        name="mla_attn",
    )(q, k, v)


def _na_kernel(q_ref, k_ref, v_ref, ck_ref, cv_ref, tab_ref, o_ref, *, rows):
    r = pl.program_id(1)
    rs = jnp.clip(r - WIN_R // 2, 0, rows - WIN_R)
    k0 = pl.multiple_of(rs * GRID_W, GRID_W)
    nloc = WIN_R * GRID_W
    tab0 = rs - r + (WIN_R - 1)
    scale = NA_DH ** -0.5
    for h in range(NA_HEADS):
        hs = slice(h * NA_DH, (h + 1) * NA_DH)
        qh = q_ref[:, hs]
        bias = jnp.concatenate([tab_ref[h, tab0 + 2 * p] for p in range(WIN_R // 2)], axis=1)
        s_loc = _bdot_nt(qh, k_ref[pl.ds(k0, nloc), hs]) * scale + bias
        s_ctx = _bdot_nt(qh, ck_ref[:, hs]) * scale
        m = jnp.maximum(jnp.max(s_loc, axis=1, keepdims=True), jnp.max(s_ctx, axis=1, keepdims=True))
        p_loc = jnp.exp(s_loc - m)
        p_ctx = jnp.exp(s_ctx - m)
        denom = jnp.sum(p_loc, axis=1, keepdims=True) + jnp.sum(p_ctx, axis=1, keepdims=True)
        o = _bdot(p_loc, v_ref[pl.ds(k0, nloc), hs]) + _bdot(p_ctx, cv_ref[:, hs])
        o_ref[:, hs] = o / denom


def _na_bias_table(rpb):
    c = np.arange(GRID_W)
    cs = np.clip(c - WIN_C // 2, 0, GRID_W - WIN_C)
    kc = np.arange(GRID_W)
    valid = (kc[None, :] >= cs[:, None]) & (kc[None, :] < cs[:, None] + WIN_C)
    coff = np.clip(kc[None, :] - c[:, None] + (WIN_C - 1), 0, 2 * WIN_C - 2)
    tab = jnp.where(valid[None, None], rpb[:, :, coff], NEG_BIG)
    return jnp.concatenate([tab[:, :-1], tab[:, 1:]], axis=-1).astype(F32)


def _na_attention(zn, ck, cv, tab, batch, seq, past):
    rows = seq // GRID_W
    return pl.pallas_call(
        functools.partial(_na_kernel, rows=rows),
        grid=(batch, rows),
        in_specs=[pl.BlockSpec((GRID_W, BRANCH_W), lambda b, r: (b * rows + r, 0)),
                  pl.BlockSpec((seq, BRANCH_W), lambda b, r: (b, 1)),
                  pl.BlockSpec((seq, BRANCH_W), lambda b, r: (b, 2)),
                  pl.BlockSpec((past, BRANCH_W), lambda b, r: (b, 0)),
                  pl.BlockSpec((past, BRANCH_W), lambda b, r: (b, 0)),
                  pl.BlockSpec(tab.shape, lambda b, r: (0, 0, 0, 0))],
        out_specs=pl.BlockSpec((GRID_W, BRANCH_W), lambda b, r: (b * rows + r, 0)),
        out_shape=jax.ShapeDtypeStruct((batch * seq, BRANCH_W), F32),
        compiler_params=_cparams(("parallel", "arbitrary")),
        name="na_attn",
    )(zn, zn, zn, ck, cv, tab)


def _rms(x, g):
    return x * lax.rsqrt(jnp.mean(x * x, axis=-1, keepdims=True) + EPS) * g


def _rotate_pairs(slab, cs):
    prod = slab * cs
    both = prod + pltpu.roll(prod, ROPE_DIM, axis=1)
    lane = lax.broadcasted_iota(jnp.int32, both.shape, 1)
    return jnp.where(lane < ROPE_DIM, both, 0.0)


def _mla_prep_kernel(*refs, has_q, norm_kv):
    if has_q:
        qd_ref, kvd_ref, kpe_ref, cs_ref, qg_ref, wq_ref, kg_ref, wk_ref, wv_ref, q_ref, k_ref, v_ref, ckv_ref = refs
    else:
        kvd_ref, kpe_ref, cs_ref, kg_ref, wk_ref, wv_ref, k_ref, v_ref, ckv_ref = refs
    cs = cs_ref[...]
    if has_q:
        scale = (NOPE_DIM + ROPE_DIM) ** -0.5
        qf = _bdot(_rms(qd_ref[...], qg_ref[...]), wq_ref[...]) * scale
        for h in range(MLA_HEADS):
            base = h * MLA_QW
            q_ref[:, base:base + NOPE_DIM] = qf[:, base:base + NOPE_DIM].astype(q_ref.dtype)
            q_ref[:, base + NOPE_DIM:base + MLA_QW] = _rotate_pairs(
                qf[:, base + NOPE_DIM:base + MLA_QW], cs).astype(q_ref.dtype)
    ckv = _rms(kvd_ref[...], kg_ref[...]) if norm_kv else kvd_ref[...]
    ckv_ref[...] = ckv
    kn = _bdot(ckv, wk_ref[...])
    vv = _bdot(ckv, wv_ref[...]).astype(v_ref.dtype)
    kr = _rotate_pairs(kpe_ref[...], cs).astype(k_ref.dtype)
    ones = jnp.ones((vv.shape[0], MLA_VW - V_DIM), v_ref.dtype)
    for h in range(MLA_HEADS):
        base = h * MLA_QW
        k_ref[:, base:base + NOPE_DIM] = kn[:, h * NOPE_DIM:(h + 1) * NOPE_DIM].astype(k_ref.dtype)
        k_ref[:, base + NOPE_DIM:base + MLA_QW] = kr
        v_ref[:, h * MLA_VW:h * MLA_VW + V_DIM] = vv[:, h * V_DIM:(h + 1) * V_DIM]
        v_ref[:, h * MLA_VW + V_DIM:(h + 1) * MLA_VW] = ones


def _mla_prep(z, cs_tab, wts, seq, has_q, norm_kv, kvd=None, kpe=None):
    qg, wq, kg, wk, wv = wts
    ntok = z.shape[0] if z is not None else kvd.shape[0]
    tm = min(512, seq)
    per_b = seq // tm
    tok = lambda i: (i, 0)
    const = lambda i: (0, 0)
    in_specs, args = [], []
    if has_q:
        in_specs.append(pl.BlockSpec((tm, Q_RANK), lambda i: (i, ZOFF['c_qd'] // Q_RANK)))
        args.append(z)
    if z is not None:
        in_specs += [pl.BlockSpec((tm, KV_RANK), lambda i: (i, ZOFF['c_kvd'] // KV_RANK)),
                     pl.BlockSpec((tm, LANE), lambda i: (i, ZOFF['c_kpe'] // LANE))]
        args += [z, z]
    else:
        in_specs += [pl.BlockSpec((tm, KV_RANK), tok), pl.BlockSpec((tm, LANE), tok)]
        args += [kvd, kpe]
    in_specs.append(pl.BlockSpec((tm, LANE), lambda i: (i % per_b, 0)))
    args.append(cs_tab)
    if has_q:
        in_specs += [pl.BlockSpec((1, Q_RANK), const), pl.BlockSpec(wq.shape, const)]
        args += [qg.reshape(1, Q_RANK), wq]
    in_specs += [pl.BlockSpec((1, KV_RANK), const), pl.BlockSpec(wk.shape, const), pl.BlockSpec(wv.shape, const)]
    args += [kg.reshape(1, KV_RANK), wk, wv]
    out_specs = [pl.BlockSpec((tm, MLA_HEADS * MLA_QW), tok), pl.BlockSpec((tm, MLA_HEADS * MLA_VW), tok),
                 pl.BlockSpec((tm, KV_RANK), tok)]
    out_shape = [jax.ShapeDtypeStruct((ntok, MLA_HEADS * MLA_QW), BF16),
                 jax.ShapeDtypeStruct((ntok, MLA_HEADS * MLA_VW), BF16),
                 jax.ShapeDtypeStruct((ntok, KV_RANK), F32)]
    if has_q:
        out_specs.insert(0, pl.BlockSpec((tm, MLA_HEADS * MLA_QW), tok))
        out_shape.insert(0, jax.ShapeDtypeStruct((ntok, MLA_HEADS * MLA_QW), BF16))
    return pl.pallas_call(
        functools.partial(_mla_prep_kernel, has_q=has_q, norm_kv=norm_kv),
        grid=(ntok // tm,),
        in_specs=in_specs, out_specs=out_specs, out_shape=out_shape,
        compiler_params=_cparams(("parallel",)),
        name="mla_prep",
    )(*args)


def _rope_table(seq):
    t = jnp.arange(seq)
    row = (t // GRID_W).astype(F32)
    col = (t % GRID_W).astype(F32)
    n_pair_axis = ROPE_DIM // 4
    inv = 1.0 / (ROPE_THETA ** (jnp.arange(n_pair_axis, dtype=F32) / n_pair_axis))
    ang = jnp.concatenate([row[:, None] * inv, col[:, None] * inv], axis=-1)
    cos, sin = jnp.cos(ang), jnp.sin(ang)
    return jnp.concatenate([cos, cos, -sin, sin], axis=-1)


def _identity_table(seq):
    return jnp.concatenate([jnp.ones((seq, ROPE_DIM), F32), jnp.zeros((seq, ROPE_DIM), F32)], axis=-1)


def _merge_kernel(x_ref, mod_ref, oa_ref, ob_ref, oc_ref, ga_ref, gb_ref, gc_ref, db_ref, dc_ref, dx_ref, dg_ref,
                  pc_ref, px_ref, nc_ref, nx_ref, wm_ref, bm_ref, wb_ref, wo_ref, cw_ref, lg_ref, lb_ref, y_ref,
                  *, per_b):
    tm = x_ref.shape[0]
    ti = pl.program_id(0) % per_b
    x = x_ref[...]
    mod = mod_ref[0]
    h = _modulate(x, mod).astype(BF16)
    gate = mod[:, 2 * D_MODEL:3 * D_MODEL]

    u = dc_ref[...] * dx_ref[...]
    prev_row = jnp.where(ti > 0, pc_ref[SUBLANE - 1:SUBLANE, :] * px_ref[SUBLANE - 1:SUBLANE, :], 0.0)
    next_row = jnp.where(ti < per_b - 1, nc_ref[0:1, :] * nx_ref[0:1, :], 0.0)
    row = lax.broadcasted_iota(jnp.int32, u.shape, 0)
    u_prev = jnp.where(row == 0, prev_row, pltpu.roll(u, 1, axis=0))
    u_next = jnp.where(row == tm - 1, next_row, pltpu.roll(u, tm - 1, axis=0))
    conv = cw_ref[0:1, :] * u_prev + cw_ref[1:2, :] * u + cw_ref[2:3, :] * u_next
    out_d = db_ref[...] * conv * _silu(dg_ref[...])

    branches = (oa_ref[...] * _silu(ga_ref[...]), ob_ref[...] * _silu(gb_ref[...]),
                oc_ref[...] * _silu(gc_ref[...]), out_d)
    mixed = jnp.zeros((tm, D_MODEL), F32)
    for n in range(N_BRANCH):
        cols = slice(n * D_MODEL, (n + 1) * D_MODEL)
        mg = jnp.dot(h, wm_ref[:, cols], preferred_element_type=F32) + bm_ref[:, cols]
        mixed = mixed + jax.nn.sigmoid(mg) * _bdot(branches[n], wb_ref[n])
    out = _bdot(mixed, wo_ref[...])
    r = DEEPNORM_ALPHA * x + gate * out
    mu = jnp.mean(r, axis=-1, keepdims=True)
    rc = r - mu
    var = jnp.mean(rc * rc, axis=-1, keepdims=True)
    y_ref[...] = rc * lax.rsqrt(var + EPS) * lg_ref[...] + lb_ref[...]


def _merge(x2, mod, z, o_a, o_b, o_c, wts, seq):
    wm, bm, wb, wo, cw, ln_g, ln_b = wts
    ntok = x2.shape[0]
    tm = min(256, seq)
    per_b = seq // tm
    nb = mod.shape[0]
    hb = tm // SUBLANE
    last_hb = ntok // SUBLANE - 1
    mod_idx = (lambda i: (i // per_b, 0, 0)) if nb > 1 else (lambda i: (0, 0, 0))
    tok = lambda i: (i, 0)
    const2 = lambda i: (0, 0)

    def zcol(name):
        blk = ZOFF[name] // BRANCH_W
        return pl.BlockSpec((tm, BRANCH_W), lambda i: (i, blk))

    def halo(name, nxt):
        blk = ZOFF[name] // BRANCH_W
        if nxt:
            return pl.BlockSpec((SUBLANE, BRANCH_W), lambda i: (jnp.minimum((i + 1) * hb, last_hb), blk))
        return pl.BlockSpec((SUBLANE, BRANCH_W), lambda i: (jnp.maximum(i * hb - 1, 0), blk))

    br = pl.BlockSpec((tm, BRANCH_W), tok)
    in_specs = [pl.BlockSpec((tm, D_MODEL), tok), pl.BlockSpec((1, 1, 3 * D_MODEL), mod_idx),
                br, br, br, zcol('a_g'), zcol('b_g'), zcol('c_g'),
                zcol('d_b'), zcol('d_c'), zcol('d_x'), zcol('d_g'),
                halo('d_c', False), halo('d_x', False), halo('d_c', True), halo('d_x', True),
                pl.BlockSpec(wm.shape, const2), pl.BlockSpec(bm.shape, const2),
                pl.BlockSpec(wb.shape, lambda i: (0, 0, 0)), pl.BlockSpec(wo.shape, const2),
                pl.BlockSpec(cw.shape, const2), pl.BlockSpec((1, D_MODEL), const2),
                pl.BlockSpec((1, D_MODEL), const2)]
    return pl.pallas_call(
        functools.partial(_merge_kernel, per_b=per_b),
        grid=(ntok // tm,),
        in_specs=in_specs,
        out_specs=pl.BlockSpec((tm, D_MODEL), tok),
        out_shape=jax.ShapeDtypeStruct((ntok, D_MODEL), F32),
        compiler_params=_cparams(("parallel",)),
        name="merge_out",
    )(x2, mod, o_a, o_b, o_c, z, z, z, z, z, z, z, z, z, z, z, wm, bm, wb, wo, cw,
      ln_g.reshape(1, D_MODEL), ln_b.reshape(1, D_MODEL))


def _layer_weights(l, w_in, b_in, lb, hg_norm_g, na_rpb, mla_qnorm_g, mla_w_qb, mla_kvnorm_g, mla_w_kvb,
                   conv_w, w_branch, w_out, ln_g, ln_b):
    w_in_l = w_in[l]
    wz = _z_columns(w_in_l).astype(BF16)
    bz = _z_columns(b_in[l]).reshape(1, ZW)
    wm = w_in_l[:, MERGE_OFF:].astype(BF16)
    bm = b_in[l, MERGE_OFF:].reshape(1, N_BRANCH * D_MODEL)
    lb_f, lb_b = lb[0, l], lb[1, l]
    lbp = jnp.stack([jnp.log(lb_f), jnp.log1p(-lb_f), jnp.log(lb_b), jnp.log1p(-lb_b)], axis=0)
    lbp = jnp.transpose(lbp.reshape(4, HG_HEADS, HG_DK), (1, 0, 2))
    wq3 = mla_w_qb[l].reshape(Q_RANK, MLA_HEADS, NOPE_DIM + ROPE_DIM)
    half = NOPE_DIM + ROPE_DIM // 2
    wq = jnp.concatenate([wq3, wq3[..., half:], wq3[..., NOPE_DIM:half]], axis=-1)
    wq = wq.reshape(Q_RANK, MLA_HEADS * MLA_QW).astype(BF16)
    wkv3 = mla_w_kvb[l].reshape(KV_RANK, MLA_HEADS, NOPE_DIM + V_DIM)
    wk = wkv3[..., :NOPE_DIM].reshape(KV_RANK, MLA_HEADS * NOPE_DIM).astype(BF16)
    wv = wkv3[..., NOPE_DIM:].reshape(KV_RANK, MLA_HEADS * V_DIM).astype(BF16)
    return dict(
        wz=wz, bz=bz, lbp=lbp, hg_norm_g=hg_norm_g[l], na_tab=_na_bias_table(na_rpb[l]),
        mla=(mla_qnorm_g[l], wq, mla_kvnorm_g[l], wk, wv),
        merge=(wm, bm, w_branch[l].astype(BF16), w_out[l].astype(BF16), conv_w[l], ln_g[l], ln_b[l]))


def _context_layer(x2, mod, w, batch, seq):
    z = _in_proj(x2, mod, w['wz'], w['bz'], seq, F32)
    s0 = jnp.zeros((batch, 2, HG_HEADS, HG_DV, HG_DK), F32)
    o_a, s_t = _hgrn(z, w['lbp'], w['hg_norm_g'], s0, batch, seq)
    cq, ck, cv = (ZOFF[n] // BRANCH_W for n in ('b_q', 'b_k', 'b_v'))
    o_b = _flash(z, z, z, batch, seq, seq, NA_HEADS, NA_DH, NA_DH, NA_DH ** -0.5, cq, ck, cv)
    q, k, v, ckv = _mla_prep(z, _identity_table(seq), w['mla'], seq, True, True)
    o_c = _mla_attention(q, k, v, batch, seq, seq)
    y = _merge(x2, mod, z, o_a, o_b, o_c, w['merge'], seq)

    def zslice(name, width):
        return z[:, ZOFF[name]:ZOFF[name] + width]

    cache = (jnp.swapaxes(s_t, -1, -2),
             zslice('b_k', BRANCH_W).reshape(batch, seq, NA_HEADS, NA_DH),
             zslice('b_v', BRANCH_W).reshape(batch, seq, NA_HEADS, NA_DH),
             ckv.reshape(batch, seq, KV_RANK),
             zslice('c_kpe', ROPE_DIM).reshape(batch, seq, ROPE_DIM))
    return y, cache


def _latent_layer(x2, mod, w, cache, batch, seq):
    s0, na_k, na_v, c_ckv, c_kpe = cache
    past = na_k.shape[1]
    z = _in_proj(x2, mod, w['wz'][:, :ZW_MAIN], w['bz'][:, :ZW_MAIN], seq, F32)
    zn = _in_proj(x2, mod, w['wz'][:, ZW_MAIN:], w['bz'][:, ZW_MAIN:], seq, BF16)
    o_a, _ = _hgrn(z, w['lbp'], w['hg_norm_g'], jnp.swapaxes(s0, -1, -2), batch, seq)
    o_b = _na_attention(zn, na_k.reshape(batch * past, BRANCH_W).astype(BF16),
                        na_v.reshape(batch * past, BRANCH_W).astype(BF16), w['na_tab'], batch, seq, past)
    q, k, v, _ = _mla_prep(z, _rope_table(seq), w['mla'], seq, True, True)
    kpe_pad = jnp.pad(c_kpe.reshape(batch * past, ROPE_DIM), ((0, 0), (0, LANE - ROPE_DIM)))
    k_c, v_c, _ = _mla_prep(None, _identity_table(past), w['mla'], past, False, False,
                            kvd=c_ckv.reshape(batch * past, KV_RANK), kpe=kpe_pad)
    k_all = jnp.concatenate([k.reshape(batch, seq, -1), k_c.reshape(batch, past, -1)], axis=1)
    v_all = jnp.concatenate([v.reshape(batch, seq, -1), v_c.reshape(batch, past, -1)], axis=1)
    tk_len = seq + past
    o_c = _mla_attention(q, k_all.reshape(batch * tk_len, -1), v_all.reshape(batch * tk_len, -1),
                         batch, seq, tk_len)
    return _merge(x2, mod, z, o_a, o_b, o_c, w['merge'], seq)


def kernel(x_prompt, x_sample, state_hgrn, cache_na_k, cache_na_v, cache_mla_ckv, cache_mla_kpe, c, c_ctx,
           w_ada, b_ada, w_in, b_in, hg_lb_logits, hg_norm_g, na_rpb, mla_qnorm_g, mla_w_qb, mla_kvnorm_g,
           mla_w_kvb, conv_w, w_branch, w_out, ln_g, ln_b):
    batch, seq, _ = x_prompt.shape
    dbatch, dseq, _ = x_sample.shape
    lb = jnp.cumsum(jax.nn.softmax(hg_lb_logits.astype(F32), axis=1), axis=1)
    lb = lb - lb[:, :1]
    n_cond = -(-(dbatch + 1) // SUBLANE) * SUBLANE
    cond = jnp.zeros((n_cond, D_MODEL), F32).at[:dbatch].set(c).at[dbatch].set(c_ctx)
    mod = _modulation(cond, w_ada, b_ada)

    y_p = x_prompt.reshape(batch * seq, D_MODEL)
    y_s = x_sample.reshape(dbatch * dseq, D_MODEL)
    caches = []
    for l in range(DEPTH):
        w = _layer_weights(l, w_in, b_in, lb, hg_norm_g, na_rpb, mla_qnorm_g, mla_w_qb, mla_kvnorm_g,
                           mla_w_kvb, conv_w, w_branch, w_out, ln_g, ln_b)
        mod_ctx = mod[l, dbatch:dbatch + 1].reshape(1, 1, 3 * D_MODEL)
        mod_lat = mod[l, :dbatch].reshape(dbatch, 1, 3 * D_MODEL)
        y_p, cache_l = _context_layer(y_p, mod_ctx, w, batch, seq)
        caches.append(cache_l)
        y_s = _latent_layer(y_s, mod_lat, w,
                            (state_hgrn[:, l], cache_na_k[:, l], cache_na_v[:, l],
                             cache_mla_ckv[:, l], cache_mla_kpe[:, l]), dbatch, dseq)
    outs = [jnp.stack([cl[i] for cl in caches], axis=1) for i in range(5)]
    return (y_p.reshape(batch, seq, D_MODEL), y_s.reshape(dbatch, dseq, D_MODEL), *outs)
```

```python
import functools

import numpy as np
import jax
import jax.numpy as jnp
from jax import lax
from jax.experimental import pallas as pl
from jax.experimental.pallas import tpu as pltpu

F32 = jnp.float32
BF16 = jnp.bfloat16

D_MODEL = 1024
DEPTH = 2
GRID_W = 64
N_BRANCH = 4
BRANCH_W = D_MODEL // 2
HG_DK = 128
HG_DV = 128
HG_HEADS = BRANCH_W // HG_DK
HG_CHUNK = 64
HG_SUB = 16
HG_NSUB = HG_CHUNK // HG_SUB
HG_SAFE_DECAY = 60.0
HG_PRE_ROWS = 512
HG_UNROLL = 4
NA_DH = 64
NA_HEADS = BRANCH_W // NA_DH
WIN_R = 8
WIN_C = 16
NA_VW = 128
NA_ROWS_PER_STEP = 4
NOPE_DIM = 128
ROPE_DIM = 64
V_DIM = 128
MLA_HEADS = BRANCH_W // V_DIM
MLA_QW = 256
MLA_VW = 256
MLA_KV_CHUNK = 1024
Q_RANK = 256
KV_RANK = 128
ROPE_THETA = 10000.0
CONV_W = 3
EPS = 1e-6
DEEPNORM_ALPHA = (2 * DEPTH) ** 0.25
NEG_BIG = -1e30

SPLIT_NAMES = ('a_q', 'a_ff', 'a_fb', 'a_i', 'a_g', 'b_q', 'b_k', 'b_v', 'b_g',
               'c_qd', 'c_kvd', 'c_kpe', 'c_g', 'd_b', 'd_c', 'd_x', 'd_g', 'merge')
SPLIT_SIZES = (BRANCH_W,) * 9 + (Q_RANK, KV_RANK, ROPE_DIM, BRANCH_W) + (BRANCH_W,) * 4 + (N_BRANCH * D_MODEL,)
_REF_OFF = dict(zip(SPLIT_NAMES, np.cumsum((0,) + SPLIT_SIZES[:-1]).tolist()))
_REF_SIZE = dict(zip(SPLIT_NAMES, SPLIT_SIZES))

_Z_ORDER = ('a_q', 'a_ff', 'a_fb', 'a_i', 'a_g', 'b_g', 'c_qd', 'c_kvd', 'c_kpe', 'c_kpe_sw', 'c_g',
            'd_b', 'd_c', 'd_x', 'd_g', 'b_q', 'b_k', 'b_v')


def _build_z_layout():
    off, pos = {}, 0
    for name in _Z_ORDER:
        off[name] = pos
        pos += _REF_SIZE['c_kpe' if name == 'c_kpe_sw' else name]
    return off, pos


ZOFF, ZW = _build_z_layout()


def _z_columns(w):
    parts = []
    for name in _Z_ORDER:
        if name == 'c_kpe_sw':
            base = _REF_OFF['c_kpe']
            parts += [w[..., base + ROPE_DIM // 2:base + ROPE_DIM], w[..., base:base + ROPE_DIM // 2]]
        else:
            parts.append(w[..., _REF_OFF[name]:_REF_OFF[name] + _REF_SIZE[name]])
    return jnp.concatenate(parts, axis=-1)


ZW_MAIN = ZOFF['b_q']
MERGE_OFF = _REF_OFF['merge']

LANE = 128
SUBLANE = 8
VMEM_LIMIT = 56 * 1024 * 1024
IN_PROJ_ROWS = 2048
IN_PROJ_COLS = 1024
BF16_SUBLANES = 16


def _cparams(sem):
    return pltpu.CompilerParams(dimension_semantics=sem, vmem_limit_bytes=VMEM_LIMIT)


def _bdot(a, b):
    return jnp.dot(a.astype(BF16), b.astype(BF16), preferred_element_type=F32)


def _bdot_nt(a, b):
    return lax.dot_general(a.astype(BF16), b.astype(BF16), (((1,), (1,)), ((), ())),
                           preferred_element_type=F32)


def _bdot_tn(a, b):
    return lax.dot_general(a.astype(BF16), b.astype(BF16), (((0,), (0,)), ((), ())),
                           preferred_element_type=F32)


def _silu(x):
    return x * jax.nn.sigmoid(x)


def _mod_kernel(c_ref, w_ref, b_ref, o_ref):
    o_ref[0] = _bdot(_silu(c_ref[...]), w_ref[0]) + b_ref[0]


def _modulation(cond, w_ada, b_ada):
    n = cond.shape[0]
    tn = D_MODEL
    return pl.pallas_call(
        _mod_kernel,
        grid=(DEPTH, 3 * D_MODEL // tn),
        in_specs=[pl.BlockSpec((n, D_MODEL), lambda l, j: (0, 0)),
                  pl.BlockSpec((1, D_MODEL, tn), lambda l, j: (l, 0, j)),
                  pl.BlockSpec((1, 1, tn), lambda l, j: (l, 0, j))],
        out_specs=pl.BlockSpec((1, n, tn), lambda l, j: (l, 0, j)),
        out_shape=jax.ShapeDtypeStruct((DEPTH, n, 3 * D_MODEL), F32),
        compiler_params=_cparams(("parallel", "parallel")),
        name="adaln_mod",
    )(cond, w_ada, b_ada.reshape(DEPTH, 1, 3 * D_MODEL))


def _modulate(x, mod_row):
    shift = mod_row[:, 0:D_MODEL]
    scale = mod_row[:, D_MODEL:2 * D_MODEL]
    return x * (1.0 + scale) + shift


def _in_proj_kernel(x_ref, mod_ref, w_ref, b_ref, o_ref, h_ref):
    @pl.when(pl.program_id(1) == 0)
    def _():
        h_ref[...] = _modulate(x_ref[...], mod_ref[0]).astype(BF16)

    o_ref[...] = (jnp.dot(h_ref[...], w_ref[...], preferred_element_type=F32) + b_ref[...]).astype(o_ref.dtype)


def _in_proj(x2, mod, w, b, seq, out_dtype):
    ntok = x2.shape[0]
    n = w.shape[1]
    nb = mod.shape[0]
    tm = min(IN_PROJ_ROWS, seq if nb > 1 else ntok)
    tn = IN_PROJ_COLS if n % IN_PROJ_COLS == 0 else IN_PROJ_COLS // 2
    per_b = seq // tm if nb > 1 else 1
    mod_idx = (lambda i, j: (i // per_b, 0, 0)) if nb > 1 else (lambda i, j: (0, 0, 0))
    return pl.pallas_call(
        _in_proj_kernel,
        grid=(ntok // tm, n // tn),
        in_specs=[pl.BlockSpec((tm, D_MODEL), lambda i, j: (i, 0)),
                  pl.BlockSpec((1, 1, 3 * D_MODEL), mod_idx),
                  pl.BlockSpec((D_MODEL, tn), lambda i, j: (0, j)),
                  pl.BlockSpec((1, tn), lambda i, j: (0, j))],
        out_specs=pl.BlockSpec((tm, tn), lambda i, j: (i, j)),
        out_shape=jax.ShapeDtypeStruct((ntok, n), out_dtype),
        scratch_shapes=[pltpu.VMEM((tm, D_MODEL), BF16)],
        compiler_params=_cparams(("parallel", "arbitrary")),
        name="in_proj",
    )(x2, mod, w, b)


def _hgrn_log_gate(fr, la, l1):
    ls = jnp.minimum(fr, 0.0) - jnp.log(1.0 + jnp.exp(-jnp.abs(fr)))
    c = l1 + ls
    return jnp.maximum(la, c) + jnp.log(1.0 + jnp.exp(-jnp.abs(la - c)))


def _chunk_cumsum(x, row_in_chunk, reverse):
    n = x.shape[0]
    k = 1
    while k < HG_CHUNK:
        if reverse:
            x = x + jnp.where(row_in_chunk < HG_CHUNK - k, pltpu.roll(x, n - k, axis=0), 0.0)
        else:
            x = x + jnp.where(row_in_chunk >= k, pltpu.roll(x, k, axis=0), 0.0)
        k *= 2
    return x


def _hgrn_chunk_factors(q, kk, b, reverse, factorised):
    C, NS = HG_SUB, HG_NSUB

    pos = [(NS - 1 - i) if reverse else i for i in range(NS)]
    blk_at = {pos[i]: i for i in range(NS)}
    end_row = [(C * i) if reverse else (C * i + C - 1) for i in range(NS)]
    e_at = [b[end_row[blk_at[p]]:end_row[blk_at[p]] + 1, :] for p in range(NS)]
    zero_row = jnp.zeros_like(e_at[0])
    b_last = e_at[NS - 1]

    def rows(fn):
        return jnp.concatenate([jnp.broadcast_to(fn(pos[i]), (C, HG_DK)) for i in range(NS)], axis=0)

    bs = rows(lambda p: e_at[p - 1] if p > 0 else zero_row)
    be = rows(lambda p: e_at[p])
    qt = q * jnp.exp(b - bs)
    kh = kk * jnp.exp(be - b)
    qc = qt * jnp.exp(bs)
    kbar = kh * jnp.exp(b_last - be)
    q2 = qt * rows(lambda p: jnp.exp(e_at[p - 1] - e_at[p - 2]) if p >= 2 else zero_row)
    q3 = qt * rows(lambda p: jnp.exp(e_at[p - 1] - e_at[p - 3]) if p >= 3 else zero_row)
    kd = kk * jnp.exp(bs - b) if factorised else None
    return dict(qs=jnp.concatenate([qt, q2, q3], axis=0).astype(BF16), qt=qt.astype(BF16), kh=kh.astype(BF16),
                kd=None if kd is None else kd.astype(BF16), qc=qc.astype(BF16), kbar=kbar.astype(BF16),
                state_decay=jnp.exp(b_last))


def _hgrn_chunk_scores(f, reverse):
    L, C = HG_CHUNK, HG_SUB
    ti = lax.broadcasted_iota(jnp.int32, (L, L), 0)
    si = lax.broadcasted_iota(jnp.int32, (L, L), 1)
    a_all = _bdot_nt(f['qs'], f['kh'])
    pt = ti // C
    ps = si // C
    gap = (ps - pt) if reverse else (pt - ps)
    a = (jnp.where(gap == 1, a_all[0:L], 0.0) + jnp.where(gap == 2, a_all[L:2 * L], 0.0)
         + jnp.where(gap == 3, a_all[2 * L:3 * L], 0.0))
    if f['kd'] is not None:
        before = (si >= ti) if reverse else (si <= ti)
        a = a + jnp.where((gap == 0) & before, _bdot_nt(f['qt'], f['kd']), 0.0)
    return a.astype(BF16)


def _hgrn_pairwise_diag(q, kk, b, v, reverse):
    C, NS = HG_SUB, HG_NSUB
    s3 = lax.broadcasted_iota(jnp.int32, (C, C, HG_DK), 0)
    t3 = lax.broadcasted_iota(jnp.int32, (C, C, HG_DK), 1)
    keep = (t3 <= s3) if reverse else (t3 >= s3)
    ones = jnp.ones((HG_DK, HG_DV), BF16)
    diag = []
    for i in range(NS):
        sl = slice(C * i, C * (i + 1))
        bb, qb, kb, vb = b[sl], q[sl], kk[sl], v[sl].astype(F32)
        dec = jnp.exp(jnp.where(keep, bb[None, :, :] - bb[:, None, :], NEG_BIG))
        x = (qb[None, :, :] * kb[:, None, :] * dec).reshape(C * C, HG_DK)
        rep = jnp.dot(x.astype(BF16), ones, preferred_element_type=F32)
        diag.append(jnp.sum(rep.reshape(C, C, HG_DV) * vb[:, None, :], axis=0))
    return jnp.concatenate(diag, axis=0)


def _hgrn_chunks(ins, states, factorised):
    fac = [_hgrn_chunk_factors(q, kk, b, rev, factorised) for q, kk, b, v, rev, _ in ins]
    att = [_hgrn_chunk_scores(f, c[4]) for f, c in zip(fac, ins)]
    intra = [_bdot(a, c[3]) for a, c in zip(att, ins)]
    upd = [_bdot_tn(c[3], f['kbar']) for f, c in zip(fac, ins)]
    if not factorised:
        intra = [o + _hgrn_pairwise_diag(q, kk, b, v, rev) for o, (q, kk, b, v, rev, _) in zip(intra, ins)]
    states = list(states)
    outs = []
    for f, c, o, u in zip(fac, ins, intra, upd):
        s_t = states[c[5]]
        outs.append(o + _bdot_nt(f['qc'], s_t))
        states[c[5]] = s_t * f['state_decay'] + u
    return outs, states


def _hgrn_kernel(q_ref, ff_ref, fb_ref, i_ref, lb_ref, ng_ref, s0_ref, o_ref, sT_ref,
                 qs_ref, kf_ref, kb_ref, bf_ref, bb_ref, of_ref, ob_ref, *, seq):
    L, C = HG_CHUNK, HG_SUB
    n = seq // L
    la_f, l1_f = lb_ref[0, 0:1, :], lb_ref[0, 1:2, :]
    la_b, l1_b = lb_ref[0, 2:3, :], lb_ref[0, 3:4, :]

    pb = min(HG_PRE_ROWS, seq)
    row_in_chunk = lax.broadcasted_iota(jnp.int32, (pb, HG_DK), 0) % L

    def pre(i, lo):
        r = pl.ds(pl.multiple_of(i * pb, pb), pb)
        qs_ref[r, :] = _silu(q_ref[r, :].astype(F32))
        lgf = _hgrn_log_gate(ff_ref[r, :].astype(F32), la_f, l1_f)
        lgb = _hgrn_log_gate(fb_ref[r, :].astype(F32), la_b, l1_b)
        kf_ref[r, :] = 1.0 - jnp.exp(lgf)
        kb_ref[r, :] = 1.0 - jnp.exp(lgb)
        bf_ref[r, :] = _chunk_cumsum(lgf, row_in_chunk, False)
        bb_ref[r, :] = _chunk_cumsum(lgb, row_in_chunk, True)
        tot = jnp.minimum(jnp.sum(lgf.reshape(pb // C, C, HG_DK), axis=1),
                          jnp.sum(lgb.reshape(pb // C, C, HG_DK), axis=1))
        return jnp.minimum(lo, jnp.min(tot, axis=0, keepdims=True))

    lo = lax.fori_loop(0, seq // pb, pre, jnp.zeros((1, HG_DK), F32))
    bounded = jnp.min(lo) > -HG_SAFE_DECAY

    def scan(factorised):
        u = min(HG_UNROLL, n) if factorised else 1

        def body(g, carry):
            s_f, s_b = carry
            rows_f = [pl.ds(pl.multiple_of((g * u + j) * L, L), L) for j in range(u)]
            rows_b = [pl.ds(pl.multiple_of((n - 1 - g * u - j) * L, L), L) for j in range(u)]
            ins = []
            for j in range(u):
                ins.append((qs_ref[rows_f[j], :], kf_ref[rows_f[j], :], bf_ref[rows_f[j], :],
                            i_ref[rows_f[j], :], False, 0))
                ins.append((qs_ref[rows_b[j], :], kb_ref[rows_b[j], :], bb_ref[rows_b[j], :],
                            i_ref[rows_b[j], :], True, 1))
            outs, (s_f, s_b) = _hgrn_chunks(ins, (s_f, s_b), factorised)
            for j in range(u):
                of_ref[rows_f[j], :] = outs[2 * j]
                ob_ref[rows_b[j], :] = outs[2 * j + 1]
            return s_f, s_b

        s_f, s_b = lax.fori_loop(0, n // u, body, (s0_ref[0, 0, 0], s0_ref[0, 1, 0]))
        sT_ref[0, 0, 0] = s_f
        sT_ref[0, 1, 0] = s_b

    pl.when(bounded)(lambda: scan(True))
    pl.when(jnp.logical_not(bounded))(lambda: scan(False))

    o = of_ref[...] + ob_ref[...]
    o_ref[...] = (o * lax.rsqrt(jnp.mean(o * o, axis=-1, keepdims=True) + EPS) * ng_ref[...]).astype(o_ref.dtype)


def _hgrn(z, lbp, norm_g, s0_t, batch, seq):
    hb = BRANCH_W // HG_DK

    def col(name):
        base = ZOFF[name] // HG_DK
        return pl.BlockSpec((seq, HG_DK), lambda b, h: (b, base + h))

    st_spec = pl.BlockSpec((1, 2, 1, HG_DV, HG_DK), lambda b, h: (b, 0, h, 0, 0))
    return pl.pallas_call(
        functools.partial(_hgrn_kernel, seq=seq),
        grid=(batch, hb),
        in_specs=[col('a_q'), col('a_ff'), col('a_fb'), col('a_i'),
                  pl.BlockSpec((1, 4, HG_DK), lambda b, h: (h, 0, 0)),
                  pl.BlockSpec((1, HG_DV), lambda b, h: (0, 0)),
                  st_spec],
        out_specs=[pl.BlockSpec((seq, HG_DV), lambda b, h: (b, h)), st_spec],
        out_shape=[jax.ShapeDtypeStruct((batch * seq, BRANCH_W), z.dtype),
                   jax.ShapeDtypeStruct((batch, 2, HG_HEADS, HG_DV, HG_DK), F32)],
        scratch_shapes=[pltpu.VMEM((seq, HG_DK), F32)] * 5 + [pltpu.VMEM((seq, HG_DV), F32)] * 2,
        compiler_params=_cparams(("parallel", "parallel")),
        name="hgrn2",
    )(z, z, z, z, lbp, norm_g.reshape(1, HG_DV), s0_t)


def _flash_kernel(q_ref, k_ref, v_ref, o_ref, m_ref, l_ref, acc_ref, *, heads, dq, dv, scale):
    j = pl.program_id(2)

    @pl.when(j == 0)
    def _():
        m_ref[...] = jnp.full(m_ref.shape, NEG_BIG, F32)
        l_ref[...] = jnp.zeros(l_ref.shape, F32)
        acc_ref[...] = jnp.zeros(acc_ref.shape, F32)

    for h in range(heads):
        s = _bdot_nt(q_ref[:, h * dq:(h + 1) * dq], k_ref[:, h * dq:(h + 1) * dq])
        if scale != 1.0:
            s = s * scale
        m_prev = m_ref[h][:, 0:1]
        m_new = jnp.maximum(m_prev, jnp.max(s, axis=1, keepdims=True))
        alpha = jnp.exp(m_prev - m_new)
        p = jnp.exp(s - m_new)
        l_new = alpha * l_ref[h][:, 0:1] + jnp.sum(p, axis=1, keepdims=True)
        hs = slice(h * dv, (h + 1) * dv)
        acc_ref[:, hs] = alpha * acc_ref[:, hs] + _bdot(p, v_ref[:, hs])
        m_ref[h] = jnp.broadcast_to(m_new, m_ref.shape[1:])
        l_ref[h] = jnp.broadcast_to(l_new, l_ref.shape[1:])

    @pl.when(j == pl.num_programs(2) - 1)
    def _():
        for h in range(heads):
            hs = slice(h * dv, (h + 1) * dv)
            o_ref[:, hs] = acc_ref[:, hs] / l_ref[h][:, 0:1]


def _flash(q, k, v, batch, tq_len, tk_len, heads, dq, dv, scale, qcol=0, kcol=0, vcol=0, tq=512, tk=256):
    tq = min(tq, tq_len)
    tk = min(tk, tk_len)
    nq, nk = tq_len // tq, tk_len // tk
    return pl.pallas_call(
        functools.partial(_flash_kernel, heads=heads, dq=dq, dv=dv, scale=scale),
        grid=(batch, nq, nk),
        in_specs=[pl.BlockSpec((tq, heads * dq), lambda b, i, j: (b * nq + i, qcol)),
                  pl.BlockSpec((tk, heads * dq), lambda b, i, j: (b * nk + j, kcol)),
                  pl.BlockSpec((tk, heads * dv), lambda b, i, j: (b * nk + j, vcol))],
        out_specs=pl.BlockSpec((tq, heads * dv), lambda b, i, j: (b * nq + i, 0)),
        out_shape=jax.ShapeDtypeStruct((batch * tq_len, heads * dv), F32),
        scratch_shapes=[pltpu.VMEM((heads, tq, LANE), F32), pltpu.VMEM((heads, tq, LANE), F32),
                        pltpu.VMEM((tq, heads * dv), F32)],
        compiler_params=_cparams(("parallel", "parallel", "arbitrary")),
        name="flash_attn",
    )(q, k, v)


def _mla_attn_kernel(q_ref, k_ref, v_ref, o_ref, *, chunks):
    tq = q_ref.shape[0]
    for h in range(MLA_HEADS):
        qs = slice(h * MLA_QW, (h + 1) * MLA_QW)
        vs = slice(h * MLA_VW, (h + 1) * MLA_VW)
        qh = q_ref[:, qs]
        m = jnp.full((tq, 1), NEG_BIG, F32)
        acc = jnp.zeros((tq, MLA_VW), F32)
        for c0, cn in chunks:
            s = _bdot_nt(qh, k_ref[c0:c0 + cn, qs])
            m_new = jnp.maximum(m, jnp.max(s, axis=1, keepdims=True))
            p = jnp.exp(s - m_new)
            acc = jnp.exp(m - m_new) * acc + _bdot(p, v_ref[c0:c0 + cn, vs])
            m = m_new
        o_ref[:, h * V_DIM:(h + 1) * V_DIM] = (acc[:, :V_DIM] / acc[:, V_DIM:2 * V_DIM]).astype(o_ref.dtype)


def _mla_attention(q, k, v, batch, tq_len, tk_len, out_dtype, tq=512):
    tq = min(tq, tq_len)
    nq = tq_len // tq
    chunks, c0 = [], 0
    while c0 < tk_len:
        cn = min(MLA_KV_CHUNK, tk_len - c0)
        chunks.append((c0, cn))
        c0 += cn
    return pl.pallas_call(
        functools.partial(_mla_attn_kernel, chunks=tuple(chunks)),
        grid=(batch, nq),
        in_specs=[pl.BlockSpec((tq, MLA_HEADS * MLA_QW), lambda b, i: (b * nq + i, 0)),
                  pl.BlockSpec((tk_len, MLA_HEADS * MLA_QW), lambda b, i: (b, 0)),
                  pl.BlockSpec((tk_len, MLA_HEADS * MLA_VW), lambda b, i: (b, 0))],
        out_specs=pl.BlockSpec((tq, MLA_HEADS * V_DIM), lambda b, i: (b * nq + i, 0)),
        out_shape=jax.ShapeDtypeStruct((batch * tq_len, MLA_HEADS * V_DIM), out_dtype),
        compiler_params=_cparams(("parallel", "arbitrary")),
        name="mla_attn",
    )(q, k, v)


def _na_kernel(q_ref, k_ref, v_ref, ck_ref, cv_ref, tab_ref, o_ref, *, rows):
    nloc = WIN_R * GRID_W
    pairs = [(j, h) for j in range(NA_ROWS_PER_STEP) for h in range(NA_HEADS)]
    k0s, tab0s = [], []
    for j in range(NA_ROWS_PER_STEP):
        r = pl.program_id(1) * NA_ROWS_PER_STEP + j
        rs = jnp.clip(r - WIN_R // 2, 0, rows - WIN_R)
        k0s.append(pl.multiple_of(rs * GRID_W, GRID_W))
        tab0s.append(rs - r + (WIN_R - 1))

    scores = []
    for j, h in pairs:
        hs = slice(h * NA_DH, (h + 1) * NA_DH)
        qh = q_ref[j * GRID_W:(j + 1) * GRID_W, hs]
        bias = jnp.concatenate([tab_ref[h, tab0s[j] + 2 * p] for p in range(WIN_R // 2)], axis=1)
        scores.append((_bdot_nt(qh, k_ref[pl.ds(k0s[j], nloc), hs]) + bias, _bdot_nt(qh, ck_ref[:, hs])))
    probs = []
    for s_loc, s_ctx in scores:
        m = jnp.maximum(jnp.max(s_loc, axis=1, keepdims=True), jnp.max(s_ctx, axis=1, keepdims=True))
        probs.append((jnp.exp(s_loc - m).astype(BF16), jnp.exp(s_ctx - m).astype(BF16)))
    outs = []
    for (j, h), (p_loc, p_ctx) in zip(pairs, probs):
        vs = slice(h * NA_VW, (h + 1) * NA_VW)
        o_aug = (jnp.dot(p_loc, v_ref[pl.ds(k0s[j], nloc), vs], preferred_element_type=F32)
                 + jnp.dot(p_ctx, cv_ref[:, vs], preferred_element_type=F32))
        outs.append(o_aug / pltpu.roll(o_aug, NA_DH, axis=1))
    lane = lax.broadcasted_iota(jnp.int32, (GRID_W, 2 * NA_DH), 1)
    for j in range(NA_ROWS_PER_STEP):
        for hp in range(NA_HEADS // 2):
            even, odd = outs[j * NA_HEADS + 2 * hp], outs[j * NA_HEADS + 2 * hp + 1]
            o_ref[j * GRID_W:(j + 1) * GRID_W, hp * 2 * NA_DH:(hp + 1) * 2 * NA_DH] = jnp.where(
                lane < NA_DH, even, pltpu.roll(odd, NA_DH, axis=1)).astype(o_ref.dtype)


def _na_bias_table(rpb):
    c = np.arange(GRID_W)
    cs = np.clip(c - WIN_C // 2, 0, GRID_W - WIN_C)
    kc = np.arange(GRID_W)
    valid = (kc[None, :] >= cs[:, None]) & (kc[None, :] < cs[:, None] + WIN_C)
    coff = np.clip(kc[None, :] - c[:, None] + (WIN_C - 1), 0, 2 * WIN_C - 2)
    tab = jnp.where(valid[None, None], rpb[:, :, coff], NEG_BIG)
    return jnp.concatenate([tab[:, :-1], tab[:, 1:]], axis=-1).astype(F32)


def _na_value_layout(v, ones_value):
    v3 = v.reshape(v.shape[:-1] + (NA_HEADS, NA_DH))
    pad = jnp.full(v3.shape[:-1] + (NA_VW - NA_DH,), ones_value, v.dtype)
    return jnp.concatenate([v3, pad], axis=-1).reshape(v.shape[:-1] + (NA_HEADS * NA_VW,))


def _na_attention(zn, col0, ck, cv, tab, batch, seq, past):
    rows = seq // GRID_W
    steps = rows // NA_ROWS_PER_STEP
    tq = NA_ROWS_PER_STEP * GRID_W
    vw = NA_HEADS * NA_VW
    qb = col0 // BRANCH_W
    vb = (col0 + 2 * BRANCH_W) // vw
    return pl.pallas_call(
        functools.partial(_na_kernel, rows=rows),
        grid=(batch, steps),
        in_specs=[pl.BlockSpec((tq, BRANCH_W), lambda b, r: (b * steps + r, qb)),
                  pl.BlockSpec((seq, BRANCH_W), lambda b, r: (b, qb + 1)),
                  pl.BlockSpec((seq, vw), lambda b, r: (b, vb)),
                  pl.BlockSpec((past, BRANCH_W), lambda b, r: (b, 0)),
                  pl.BlockSpec((past, vw), lambda b, r: (b, 0)),
                  pl.BlockSpec(tab.shape, lambda b, r: (0, 0, 0, 0))],
        out_specs=pl.BlockSpec((tq, BRANCH_W), lambda b, r: (b * steps + r, 0)),
        out_shape=jax.ShapeDtypeStruct((batch * seq, BRANCH_W), zn.dtype),
        compiler_params=_cparams(("parallel", "arbitrary")),
        name="na_attn",
    )(zn, zn, zn, ck, cv, tab)


def _rms(x, g):
    return x * lax.rsqrt(jnp.mean(x * x, axis=-1, keepdims=True) + EPS) * g


def _rotate_pairs(slab, cs):
    prod = slab * cs
    both = prod + pltpu.roll(prod, ROPE_DIM, axis=1)
    lane = lax.broadcasted_iota(jnp.int32, both.shape, 1)
    return jnp.where(lane < ROPE_DIM, both, 0.0)


def _mla_prep_kernel(*refs, has_q, norm_kv):
    if has_q:
        qd_ref, kvd_ref, kpe_ref, cs_ref, qg_ref, wq_ref, kg_ref, wk_ref, wv_ref, q_ref, k_ref, v_ref, ckv_ref = refs
    else:
        kvd_ref, kpe_ref, cs_ref, kg_ref, wk_ref, wv_ref, k_ref, v_ref, ckv_ref = refs
    cs = cs_ref[...]
    if has_q:
        scale = (NOPE_DIM + ROPE_DIM) ** -0.5
        qf = _bdot(_rms(qd_ref[...].astype(F32), qg_ref[...]), wq_ref[...]) * scale
        for h in range(MLA_HEADS):
            base = h * MLA_QW
            q_ref[:, base:base + NOPE_DIM] = qf[:, base:base + NOPE_DIM].astype(q_ref.dtype)
            q_ref[:, base + NOPE_DIM:base + MLA_QW] = _rotate_pairs(
                qf[:, base + NOPE_DIM:base + MLA_QW], cs).astype(q_ref.dtype)
    kvd = kvd_ref[...].astype(F32)
    ckv = _rms(kvd, kg_ref[...]) if norm_kv else kvd
    ckv_ref[...] = ckv
    kn = _bdot(ckv, wk_ref[...])
    vv = _bdot(ckv, wv_ref[...]).astype(v_ref.dtype)
    kr = _rotate_pairs(kpe_ref[...].astype(F32), cs).astype(k_ref.dtype)
    ones = jnp.ones((vv.shape[0], MLA_VW - V_DIM), v_ref.dtype)
    for h in range(MLA_HEADS):
        base = h * MLA_QW
        k_ref[:, base:base + NOPE_DIM] = kn[:, h * NOPE_DIM:(h + 1) * NOPE_DIM].astype(k_ref.dtype)
        k_ref[:, base + NOPE_DIM:base + MLA_QW] = kr
        v_ref[:, h * MLA_VW:h * MLA_VW + V_DIM] = vv[:, h * V_DIM:(h + 1) * V_DIM]
        v_ref[:, h * MLA_VW + V_DIM:(h + 1) * MLA_VW] = ones


def _mla_prep(z, cs_tab, wts, seq, has_q, norm_kv, kvd=None, kpe=None):
    qg, wq, kg, wk, wv = wts
    ntok = z.shape[0] if z is not None else kvd.shape[0]
    tm = min(512, seq)
    per_b = seq // tm
    tok = lambda i: (i, 0)
    const = lambda i: (0, 0)
    in_specs, args = [], []
    if has_q:
        in_specs.append(pl.BlockSpec((tm, Q_RANK), lambda i: (i, ZOFF['c_qd'] // Q_RANK)))
        args.append(z)
    if z is not None:
        in_specs += [pl.BlockSpec((tm, KV_RANK), lambda i: (i, ZOFF['c_kvd'] // KV_RANK)),
                     pl.BlockSpec((tm, LANE), lambda i: (i, ZOFF['c_kpe'] // LANE))]
        args += [z, z]
    else:
        in_specs += [pl.BlockSpec((tm, KV_RANK), tok), pl.BlockSpec((tm, LANE), tok)]
        args += [kvd, kpe]
    in_specs.append(pl.BlockSpec((tm, LANE), lambda i: (i % per_b, 0)))
    args.append(cs_tab)
    if has_q:
        in_specs += [pl.BlockSpec((1, Q_RANK), const), pl.BlockSpec(wq.shape, const)]
        args += [qg.reshape(1, Q_RANK), wq]
    in_specs += [pl.BlockSpec((1, KV_RANK), const), pl.BlockSpec(wk.shape, const), pl.BlockSpec(wv.shape, const)]
    args += [kg.reshape(1, KV_RANK), wk, wv]
    out_specs = [pl.BlockSpec((tm, MLA_HEADS * MLA_QW), tok), pl.BlockSpec((tm, MLA_HEADS * MLA_VW), tok),
                 pl.BlockSpec((tm, KV_RANK), tok)]
    out_shape = [jax.ShapeDtypeStruct((ntok, MLA_HEADS * MLA_QW), BF16),
                 jax.ShapeDtypeStruct((ntok, MLA_HEADS * MLA_VW), BF16),
                 jax.ShapeDtypeStruct((ntok, KV_RANK), F32)]
    if has_q:
        out_specs.insert(0, pl.BlockSpec((tm, MLA_HEADS * MLA_QW), tok))
        out_shape.insert(0, jax.ShapeDtypeStruct((ntok, MLA_HEADS * MLA_QW), BF16))
    return pl.pallas_call(
        functools.partial(_mla_prep_kernel, has_q=has_q, norm_kv=norm_kv),
        grid=(ntok // tm,),
        in_specs=in_specs, out_specs=out_specs, out_shape=out_shape,
        compiler_params=_cparams(("parallel",)),
        name="mla_prep",
    )(*args)


def _rope_table(seq):
    t = jnp.arange(seq)
    row = (t // GRID_W).astype(F32)
    col = (t % GRID_W).astype(F32)
    n_pair_axis = ROPE_DIM // 4
    inv = 1.0 / (ROPE_THETA ** (jnp.arange(n_pair_axis, dtype=F32) / n_pair_axis))
    ang = jnp.concatenate([row[:, None] * inv, col[:, None] * inv], axis=-1)
    cos, sin = jnp.cos(ang), jnp.sin(ang)
    return jnp.concatenate([cos, cos, -sin, sin], axis=-1)


def _identity_table(seq):
    return jnp.concatenate([jnp.ones((seq, ROPE_DIM), F32), jnp.zeros((seq, ROPE_DIM), F32)], axis=-1)


def _merge_kernel(x_ref, mod_ref, oa_ref, ob_ref, oc_ref, ga_ref, gb_ref, gc_ref, db_ref, dc_ref, dx_ref, dg_ref,
                  pc_ref, px_ref, nc_ref, nx_ref, wm_ref, bm_ref, wb_ref, wo_ref, cw_ref, lg_ref, lb_ref, y_ref,
                  *, per_b):
    tm = x_ref.shape[0]
    ti = pl.program_id(0) % per_b
    x = x_ref[...]
    mod = mod_ref[0]
    h = _modulate(x, mod).astype(BF16)
    gate = mod[:, 2 * D_MODEL:3 * D_MODEL]

    def f32(ref):
        return ref[...].astype(F32)

    u = f32(dc_ref) * f32(dx_ref)
    last = pc_ref.shape[0] - 1
    prev_row = jnp.where(ti > 0, f32(pc_ref)[last:last + 1, :] * f32(px_ref)[last:last + 1, :], 0.0)
    next_row = jnp.where(ti < per_b - 1, f32(nc_ref)[0:1, :] * f32(nx_ref)[0:1, :], 0.0)
    row = lax.broadcasted_iota(jnp.int32, u.shape, 0)
    u_prev = jnp.where(row == 0, prev_row, pltpu.roll(u, 1, axis=0))
    u_next = jnp.where(row == tm - 1, next_row, pltpu.roll(u, tm - 1, axis=0))
    conv = cw_ref[0:1, :] * u_prev + cw_ref[1:2, :] * u + cw_ref[2:3, :] * u_next
    out_d = f32(db_ref) * conv * _silu(f32(dg_ref))

    branches = (f32(oa_ref) * _silu(f32(ga_ref)), f32(ob_ref) * _silu(f32(gb_ref)),
                f32(oc_ref) * _silu(f32(gc_ref)), out_d)
    mixed = jnp.zeros((tm, D_MODEL), F32)
    for n in range(N_BRANCH):
        cols = slice(n * D_MODEL, (n + 1) * D_MODEL)
        mg = jnp.dot(h, wm_ref[:, cols], preferred_element_type=F32) + bm_ref[:, cols]
        mixed = mixed + jax.nn.sigmoid(mg) * _bdot(branches[n], wb_ref[n])
    out = _bdot(mixed, wo_ref[...])
    r = DEEPNORM_ALPHA * x + gate * out
    mu = jnp.mean(r, axis=-1, keepdims=True)
    rc = r - mu
    var = jnp.mean(rc * rc, axis=-1, keepdims=True)
    y_ref[...] = rc * lax.rsqrt(var + EPS) * lg_ref[...] + lb_ref[...]


def _merge(x2, mod, z, o_a, o_b, o_c, wts, seq):
    wm, bm, wb, wo, cw, ln_g, ln_b = wts
    ntok = x2.shape[0]
    tm = min(256, seq)
    per_b = seq // tm
    nb = mod.shape[0]
    halo_rows = BF16_SUBLANES if z.dtype == BF16 else SUBLANE
    hb = tm // halo_rows
    last_hb = ntok // halo_rows - 1
    mod_idx = (lambda i: (i // per_b, 0, 0)) if nb > 1 else (lambda i: (0, 0, 0))
    tok = lambda i: (i, 0)
    const2 = lambda i: (0, 0)

    def zcol(name):
        blk = ZOFF[name] // BRANCH_W
        return pl.BlockSpec((tm, BRANCH_W), lambda i: (i, blk))

    def halo(name, nxt):
        blk = ZOFF[name] // BRANCH_W
        if nxt:
            return pl.BlockSpec((halo_rows, BRANCH_W), lambda i: (jnp.minimum((i + 1) * hb, last_hb), blk))
        return pl.BlockSpec((halo_rows, BRANCH_W), lambda i: (jnp.maximum(i * hb - 1, 0), blk))

    br = pl.BlockSpec((tm, BRANCH_W), tok)
    in_specs = [pl.BlockSpec((tm, D_MODEL), tok), pl.BlockSpec((1, 1, 3 * D_MODEL), mod_idx),
                br, br, br, zcol('a_g'), zcol('b_g'), zcol('c_g'),
                zcol('d_b'), zcol('d_c'), zcol('d_x'), zcol('d_g'),
                halo('d_c', False), halo('d_x', False), halo('d_c', True), halo('d_x', True),
                pl.BlockSpec(wm.shape, const2), pl.BlockSpec(bm.shape, const2),
                pl.BlockSpec(wb.shape, lambda i: (0, 0, 0)), pl.BlockSpec(wo.shape, const2),
                pl.BlockSpec(cw.shape, const2), pl.BlockSpec((1, D_MODEL), const2),
                pl.BlockSpec((1, D_MODEL), const2)]
    return pl.pallas_call(
        functools.partial(_merge_kernel, per_b=per_b),
        grid=(ntok // tm,),
        in_specs=in_specs,
        out_specs=pl.BlockSpec((tm, D_MODEL), tok),
        out_shape=jax.ShapeDtypeStruct((ntok, D_MODEL), F32),
        compiler_params=_cparams(("parallel",)),
        name="merge_out",
    )(x2, mod, o_a, o_b, o_c, z, z, z, z, z, z, z, z, z, z, z, wm, bm, wb, wo, cw,
      ln_g.reshape(1, D_MODEL), ln_b.reshape(1, D_MODEL))


def _layer_weights(l, w_in, b_in, lb, hg_norm_g, na_rpb, mla_qnorm_g, mla_w_qb, mla_kvnorm_g, mla_w_kvb,
                   conv_w, w_branch, w_out, ln_g, ln_b):
    w_in_l = w_in[l]
    wz = _z_columns(w_in_l).astype(BF16)
    bz = _z_columns(b_in[l]).reshape(1, ZW)
    wm = w_in_l[:, MERGE_OFF:].astype(BF16)
    bm = b_in[l, MERGE_OFF:].reshape(1, N_BRANCH * D_MODEL)
    def ref_cols(a, name):
        return a[..., _REF_OFF[name]:_REF_OFF[name] + _REF_SIZE[name]]

    na_scale = NA_DH ** -0.5
    wn = jnp.concatenate([ref_cols(w_in_l, 'b_q') * na_scale, ref_cols(w_in_l, 'b_k'),
                          _na_value_layout(ref_cols(w_in_l, 'b_v'), 0.0)], axis=-1).astype(BF16)
    bn = jnp.concatenate([ref_cols(b_in[l], 'b_q') * na_scale, ref_cols(b_in[l], 'b_k'),
                          _na_value_layout(ref_cols(b_in[l], 'b_v'), 1.0)], axis=-1).reshape(1, -1)
    lb_f, lb_b = lb[0, l], lb[1, l]
    lbp = jnp.stack([jnp.log(lb_f), jnp.log1p(-lb_f), jnp.log(lb_b), jnp.log1p(-lb_b)], axis=0)
    lbp = jnp.transpose(lbp.reshape(4, HG_HEADS, HG_DK), (1, 0, 2))
    wq3 = mla_w_qb[l].reshape(Q_RANK, MLA_HEADS, NOPE_DIM + ROPE_DIM)
    half = NOPE_DIM + ROPE_DIM // 2
    wq = jnp.concatenate([wq3, wq3[..., half:], wq3[..., NOPE_DIM:half]], axis=-1)
    wq = wq.reshape(Q_RANK, MLA_HEADS * MLA_QW).astype(BF16)
    wkv3 = mla_w_kvb[l].reshape(KV_RANK, MLA_HEADS, NOPE_DIM + V_DIM)
    wk = wkv3[..., :NOPE_DIM].reshape(KV_RANK, MLA_HEADS * NOPE_DIM).astype(BF16)
    wv = wkv3[..., NOPE_DIM:].reshape(KV_RANK, MLA_HEADS * V_DIM).astype(BF16)
    return dict(
        wz=wz, bz=bz, wn=wn, bn=bn, lbp=lbp, hg_norm_g=hg_norm_g[l], na_tab=_na_bias_table(na_rpb[l]),
        mla=(mla_qnorm_g[l], wq, mla_kvnorm_g[l], wk, wv),
        merge=(wm, bm, w_branch[l].astype(BF16), w_out[l].astype(BF16), conv_w[l], ln_g[l], ln_b[l]))


def _context_layer(x2, mod, w, batch, seq):
    z = _in_proj(x2, mod, w['wz'], w['bz'], seq, F32)
    s0 = jnp.zeros((batch, 2, HG_HEADS, HG_DV, HG_DK), F32)
    o_a, s_t = _hgrn(z, w['lbp'], w['hg_norm_g'], s0, batch, seq)
    cq, ck, cv = (ZOFF[n] // BRANCH_W for n in ('b_q', 'b_k', 'b_v'))
    o_b = _flash(z, z, z, batch, seq, seq, NA_HEADS, NA_DH, NA_DH, NA_DH ** -0.5, cq, ck, cv)
    q, k, v, ckv = _mla_prep(z, _identity_table(seq), w['mla'], seq, True, True)
    o_c = _mla_attention(q, k, v, batch, seq, seq, F32)
    y = _merge(x2, mod, z, o_a, o_b, o_c, w['merge'], seq)

    def zslice(name, width):
        return z[:, ZOFF[name]:ZOFF[name] + width]

    cache = (jnp.swapaxes(s_t, -1, -2),
             zslice('b_k', BRANCH_W).reshape(batch, seq, NA_HEADS, NA_DH),
             zslice('b_v', BRANCH_W).reshape(batch, seq, NA_HEADS, NA_DH),
             ckv.reshape(batch, seq, KV_RANK),
             zslice('c_kpe', ROPE_DIM).reshape(batch, seq, ROPE_DIM))
    return y, cache


def _latent_layer(x2, mod, w, cache, batch, seq):
    s0, na_k, na_v, c_ckv, c_kpe = cache
    past = na_k.shape[1]
    z = _in_proj(x2, mod, jnp.concatenate([w['wz'][:, :ZW_MAIN], w['wn']], axis=1),
                 jnp.concatenate([w['bz'][:, :ZW_MAIN], w['bn']], axis=1), seq, BF16)
    o_a, _ = _hgrn(z, w['lbp'], w['hg_norm_g'], jnp.swapaxes(s0, -1, -2), batch, seq)
    o_b = _na_attention(z, ZW_MAIN, na_k.reshape(batch * past, BRANCH_W).astype(BF16),
                        _na_value_layout(na_v.reshape(batch * past, BRANCH_W), 1.0).astype(BF16),
                        w['na_tab'], batch, seq, past)
    q, k, v, _ = _mla_prep(z, _rope_table(seq), w['mla'], seq, True, True)
    kpe_pad = jnp.pad(c_kpe.reshape(batch * past, ROPE_DIM), ((0, 0), (0, LANE - ROPE_DIM)))
    k_c, v_c, _ = _mla_prep(None, _identity_table(past), w['mla'], past, False, False,
                            kvd=c_ckv.reshape(batch * past, KV_RANK), kpe=kpe_pad)
    k_all = jnp.concatenate([k.reshape(batch, seq, -1), k_c.reshape(batch, past, -1)], axis=1)
    v_all = jnp.concatenate([v.reshape(batch, seq, -1), v_c.reshape(batch, past, -1)], axis=1)
    tk_len = seq + past
    o_c = _mla_attention(q, k_all.reshape(batch * tk_len, -1), v_all.reshape(batch * tk_len, -1),
                         batch, seq, tk_len, BF16)
    return _merge(x2, mod, z, o_a, o_b, o_c, w['merge'], seq)


def kernel(x_prompt, x_sample, state_hgrn, cache_na_k, cache_na_v, cache_mla_ckv, cache_mla_kpe, c, c_ctx,
           w_ada, b_ada, w_in, b_in, hg_lb_logits, hg_norm_g, na_rpb, mla_qnorm_g, mla_w_qb, mla_kvnorm_g,
           mla_w_kvb, conv_w, w_branch, w_out, ln_g, ln_b):
    batch, seq, _ = x_prompt.shape
    dbatch, dseq, _ = x_sample.shape
    lb = jnp.cumsum(jax.nn.softmax(hg_lb_logits.astype(F32), axis=1), axis=1)
    lb = lb - lb[:, :1]
    n_cond = -(-(dbatch + 1) // SUBLANE) * SUBLANE
    cond = jnp.zeros((n_cond, D_MODEL), F32).at[:dbatch].set(c).at[dbatch].set(c_ctx)
    mod = _modulation(cond, w_ada, b_ada)

    y_p = x_prompt.reshape(batch * seq, D_MODEL)
    y_s = x_sample.reshape(dbatch * dseq, D_MODEL)
    caches = []
    for l in range(DEPTH):
        w = _layer_weights(l, w_in, b_in, lb, hg_norm_g, na_rpb, mla_qnorm_g, mla_w_qb, mla_kvnorm_g,
                           mla_w_kvb, conv_w, w_branch, w_out, ln_g, ln_b)
        mod_ctx = mod[l, dbatch:dbatch + 1].reshape(1, 1, 3 * D_MODEL)
        mod_lat = mod[l, :dbatch].reshape(dbatch, 1, 3 * D_MODEL)
        y_p, cache_l = _context_layer(y_p, mod_ctx, w, batch, seq)
        caches.append(cache_l)
        y_s = _latent_layer(y_s, mod_lat, w,
                            (state_hgrn[:, l], cache_na_k[:, l], cache_na_v[:, l],
                             cache_mla_ckv[:, l], cache_mla_kpe[:, l]), dbatch, dseq)
    outs = [jnp.stack([cl[i] for cl in caches], axis=1) for i in range(5)]
    return (y_p.reshape(batch, seq, D_MODEL), y_s.reshape(dbatch, dseq, D_MODEL), *outs)
```

```python
import functools

import numpy as np
import jax
import jax.numpy as jnp
from jax import lax
from jax.experimental import pallas as pl
from jax.experimental.pallas import tpu as pltpu

F32 = jnp.float32
BF16 = jnp.bfloat16

D_MODEL = 1024
DEPTH = 2
GRID_W = 64
N_BRANCH = 4
BRANCH_W = D_MODEL // 2
HG_DK = 128
HG_DV = 128
HG_HEADS = BRANCH_W // HG_DK
HG_CHUNK = 64
HG_SUB = 16
HG_NSUB = HG_CHUNK // HG_SUB
HG_SAFE_DECAY = 60.0
HG_PRE_ROWS = 512
HG_UNROLL = 4
NA_DH = 64
NA_HEADS = BRANCH_W // NA_DH
WIN_R = 8
WIN_C = 16
NA_VW = 128
NA_ROWS_PER_STEP = 4
NOPE_DIM = 128
ROPE_DIM = 64
V_DIM = 128
MLA_HEADS = BRANCH_W // V_DIM
MLA_QW = 256
MLA_VW = 256
MLA_KV_CHUNK = 1024
MLA_PREP_ROWS = 256
Q_RANK = 256
KV_RANK = 128
ROPE_THETA = 10000.0
CONV_W = 3
EPS = 1e-6
DEEPNORM_ALPHA = (2 * DEPTH) ** 0.25
NEG_BIG = -1e30

SPLIT_NAMES = ('a_q', 'a_ff', 'a_fb', 'a_i', 'a_g', 'b_q', 'b_k', 'b_v', 'b_g',
               'c_qd', 'c_kvd', 'c_kpe', 'c_g', 'd_b', 'd_c', 'd_x', 'd_g', 'merge')
SPLIT_SIZES = (BRANCH_W,) * 9 + (Q_RANK, KV_RANK, ROPE_DIM, BRANCH_W) + (BRANCH_W,) * 4 + (N_BRANCH * D_MODEL,)
_REF_OFF = dict(zip(SPLIT_NAMES, np.cumsum((0,) + SPLIT_SIZES[:-1]).tolist()))
_REF_SIZE = dict(zip(SPLIT_NAMES, SPLIT_SIZES))

_Z_ORDER = ('a_q', 'a_ff', 'a_fb', 'a_i', 'a_g', 'b_g', 'c_qd', 'c_kvd', 'c_kpe', 'c_kpe_sw', 'c_g',
            'd_b', 'd_c', 'd_x', 'd_g', 'b_q', 'b_k', 'b_v')


def _build_z_layout():
    off, pos = {}, 0
    for name in _Z_ORDER:
        off[name] = pos
        pos += _REF_SIZE['c_kpe' if name == 'c_kpe_sw' else name]
    return off, pos


ZOFF, ZW = _build_z_layout()


def _z_columns(w, order=_Z_ORDER):
    parts = []
    for name in order:
        if name == 'c_kpe_sw':
            base = _REF_OFF['c_kpe']
            parts += [w[..., base + ROPE_DIM // 2:base + ROPE_DIM], w[..., base:base + ROPE_DIM // 2]]
        else:
            parts.append(w[..., _REF_OFF[name]:_REF_OFF[name] + _REF_SIZE[name]])
    return parts


ZW_MAIN = ZOFF['b_q']
MERGE_OFF = _REF_OFF['merge']

LANE = 128
SUBLANE = 8
VMEM_LIMIT = 56 * 1024 * 1024
IN_PROJ_ROWS = 2048
IN_PROJ_COLS = 1024
BF16_SUBLANES = 16


def _cparams(sem):
    return pltpu.CompilerParams(dimension_semantics=sem, vmem_limit_bytes=VMEM_LIMIT)


def _bdot(a, b):
    return jnp.dot(a.astype(BF16), b.astype(BF16), preferred_element_type=F32)


def _bdot_nt(a, b):
    return lax.dot_general(a.astype(BF16), b.astype(BF16), (((1,), (1,)), ((), ())),
                           preferred_element_type=F32)


def _bdot_tn(a, b):
    return lax.dot_general(a.astype(BF16), b.astype(BF16), (((0,), (0,)), ((), ())),
                           preferred_element_type=F32)


def _silu(x):
    return x * jax.nn.sigmoid(x)


def _mod_kernel(c_ref, w_ref, b_ref, o_ref):
    o_ref[0] = _bdot(_silu(c_ref[...]), w_ref[0]) + b_ref[0]


def _modulation(cond, w_ada, b_ada):
    n = cond.shape[0]
    tn = D_MODEL
    return pl.pallas_call(
        _mod_kernel,
        grid=(DEPTH, 3 * D_MODEL // tn),
        in_specs=[pl.BlockSpec((n, D_MODEL), lambda l, j: (0, 0)),
                  pl.BlockSpec((1, D_MODEL, tn), lambda l, j: (l, 0, j)),
                  pl.BlockSpec((1, 1, tn), lambda l, j: (l, 0, j))],
        out_specs=pl.BlockSpec((1, n, tn), lambda l, j: (l, 0, j)),
        out_shape=jax.ShapeDtypeStruct((DEPTH, n, 3 * D_MODEL), F32),
        compiler_params=_cparams(("parallel", "parallel")),
        name="adaln_mod",
    )(cond, w_ada, b_ada.reshape(DEPTH, 1, 3 * D_MODEL))


def _modulate(x, mod_row):
    shift = mod_row[:, 0:D_MODEL]
    scale = mod_row[:, D_MODEL:2 * D_MODEL]
    return x * (1.0 + scale) + shift


def _in_proj_kernel(x_ref, mod_ref, w_ref, b_ref, o_ref, h_ref):
    @pl.when(pl.program_id(1) == 0)
    def _():
        h_ref[...] = _modulate(x_ref[...], mod_ref[0]).astype(BF16)

    o_ref[...] = (jnp.dot(h_ref[...], w_ref[...], preferred_element_type=F32) + b_ref[...]).astype(o_ref.dtype)


def _in_proj(x2, mod, w, b, seq, out_dtype):
    ntok = x2.shape[0]
    n = w.shape[1]
    nb = mod.shape[0]
    tm = min(IN_PROJ_ROWS, seq if nb > 1 else ntok)
    tn = IN_PROJ_COLS if n % IN_PROJ_COLS == 0 else IN_PROJ_COLS // 2
    per_b = seq // tm if nb > 1 else 1
    mod_idx = (lambda i, j: (i // per_b, 0, 0)) if nb > 1 else (lambda i, j: (0, 0, 0))
    return pl.pallas_call(
        _in_proj_kernel,
        grid=(ntok // tm, n // tn),
        in_specs=[pl.BlockSpec((tm, D_MODEL), lambda i, j: (i, 0)),
                  pl.BlockSpec((1, 1, 3 * D_MODEL), mod_idx),
                  pl.BlockSpec((D_MODEL, tn), lambda i, j: (0, j)),
                  pl.BlockSpec((1, tn), lambda i, j: (0, j))],
        out_specs=pl.BlockSpec((tm, tn), lambda i, j: (i, j)),
        out_shape=jax.ShapeDtypeStruct((ntok, n), out_dtype),
        scratch_shapes=[pltpu.VMEM((tm, D_MODEL), BF16)],
        compiler_params=_cparams(("parallel", "arbitrary")),
        name="in_proj",
    )(x2, mod, w, b)


def _hgrn_log_gate(fr, la, l1):
    ls = jnp.minimum(fr, 0.0) - jnp.log(1.0 + jnp.exp(-jnp.abs(fr)))
    c = l1 + ls
    return jnp.maximum(la, c) + jnp.log(1.0 + jnp.exp(-jnp.abs(la - c)))


def _chunk_cumsum(x, row_in_chunk, reverse):
    n = x.shape[0]
    k = 1
    while k < HG_CHUNK:
        if reverse:
            x = x + jnp.where(row_in_chunk < HG_CHUNK - k, pltpu.roll(x, n - k, axis=0), 0.0)
        else:
            x = x + jnp.where(row_in_chunk >= k, pltpu.roll(x, k, axis=0), 0.0)
        k *= 2
    return x


def _hgrn_chunk_factors(q, kk, b, reverse, factorised):
    C, NS = HG_SUB, HG_NSUB

    pos = [(NS - 1 - i) if reverse else i for i in range(NS)]
    blk_at = {pos[i]: i for i in range(NS)}
    end_row = [(C * i) if reverse else (C * i + C - 1) for i in range(NS)]
    e_at = [b[end_row[blk_at[p]]:end_row[blk_at[p]] + 1, :] for p in range(NS)]
    zero_row = jnp.zeros_like(e_at[0])
    b_last = e_at[NS - 1]

    def rows(fn):
        return jnp.concatenate([jnp.broadcast_to(fn(pos[i]), (C, HG_DK)) for i in range(NS)], axis=0)

    bs = rows(lambda p: e_at[p - 1] if p > 0 else zero_row)
    be = rows(lambda p: e_at[p])
    qt = q * jnp.exp(b - bs)
    kh = kk * jnp.exp(be - b)
    qc = qt * jnp.exp(bs)
    kbar = kh * jnp.exp(b_last - be)
    q2 = qt * rows(lambda p: jnp.exp(e_at[p - 1] - e_at[p - 2]) if p >= 2 else zero_row)
    q3 = qt * rows(lambda p: jnp.exp(e_at[p - 1] - e_at[p - 3]) if p >= 3 else zero_row)
    kd = kk * jnp.exp(bs - b) if factorised else None
    return dict(qs=jnp.concatenate([qt, q2, q3], axis=0).astype(BF16), qt=qt.astype(BF16), kh=kh.astype(BF16),
                kd=None if kd is None else kd.astype(BF16), qc=qc.astype(BF16), kbar=kbar.astype(BF16),
                state_decay=jnp.exp(b_last))


def _hgrn_chunk_scores(f, reverse):
    L, C = HG_CHUNK, HG_SUB
    ti = lax.broadcasted_iota(jnp.int32, (L, L), 0)
    si = lax.broadcasted_iota(jnp.int32, (L, L), 1)
    a_all = _bdot_nt(f['qs'], f['kh'])
    pt = ti // C
    ps = si // C
    gap = (ps - pt) if reverse else (pt - ps)
    a = (jnp.where(gap == 1, a_all[0:L], 0.0) + jnp.where(gap == 2, a_all[L:2 * L], 0.0)
         + jnp.where(gap == 3, a_all[2 * L:3 * L], 0.0))
    if f['kd'] is not None:
        before = (si >= ti) if reverse else (si <= ti)
        a = a + jnp.where((gap == 0) & before, _bdot_nt(f['qt'], f['kd']), 0.0)
    return a.astype(BF16)


def _hgrn_pairwise_diag(q, kk, b, v, reverse):
    C, NS = HG_SUB, HG_NSUB
    s3 = lax.broadcasted_iota(jnp.int32, (C, C, HG_DK), 0)
    t3 = lax.broadcasted_iota(jnp.int32, (C, C, HG_DK), 1)
    keep = (t3 <= s3) if reverse else (t3 >= s3)
    ones = jnp.ones((HG_DK, HG_DV), BF16)
    diag = []
    for i in range(NS):
        sl = slice(C * i, C * (i + 1))
        bb, qb, kb, vb = b[sl], q[sl], kk[sl], v[sl].astype(F32)
        dec = jnp.exp(jnp.where(keep, bb[None, :, :] - bb[:, None, :], NEG_BIG))
        x = (qb[None, :, :] * kb[:, None, :] * dec).reshape(C * C, HG_DK)
        rep = jnp.dot(x.astype(BF16), ones, preferred_element_type=F32)
        diag.append(jnp.sum(rep.reshape(C, C, HG_DV) * vb[:, None, :], axis=0))
    return jnp.concatenate(diag, axis=0)


def _hgrn_chunks(ins, states, factorised):
    fac = [_hgrn_chunk_factors(q, kk, b, rev, factorised) for q, kk, b, v, rev, _ in ins]
    att = [_hgrn_chunk_scores(f, c[4]) for f, c in zip(fac, ins)]
    intra = [_bdot(a, c[3]) for a, c in zip(att, ins)]
    upd = [_bdot_tn(c[3], f['kbar']) for f, c in zip(fac, ins)]
    if not factorised:
        intra = [o + _hgrn_pairwise_diag(q, kk, b, v, rev) for o, (q, kk, b, v, rev, _) in zip(intra, ins)]
    states = list(states)
    outs = []
    for f, c, o, u in zip(fac, ins, intra, upd):
        s_t = states[c[5]]
        outs.append(o + _bdot_nt(f['qc'], s_t))
        states[c[5]] = s_t * f['state_decay'] + u
    return outs, states


def _hgrn_kernel(q_ref, ff_ref, fb_ref, i_ref, lb_ref, ng_ref, s0_ref, o_ref, sT_ref,
                 qs_ref, kf_ref, kb_ref, bf_ref, bb_ref, of_ref, ob_ref, *, seq):
    L, C = HG_CHUNK, HG_SUB
    n = seq // L
    la_f, l1_f = lb_ref[0, 0:1, :], lb_ref[0, 1:2, :]
    la_b, l1_b = lb_ref[0, 2:3, :], lb_ref[0, 3:4, :]

    pb = min(HG_PRE_ROWS, seq)
    row_in_chunk = lax.broadcasted_iota(jnp.int32, (pb, HG_DK), 0) % L

    def pre(i, lo):
        r = pl.ds(pl.multiple_of(i * pb, pb), pb)
        qs_ref[r, :] = _silu(q_ref[r, :].astype(F32))
        lgf = _hgrn_log_gate(ff_ref[r, :].astype(F32), la_f, l1_f)
        lgb = _hgrn_log_gate(fb_ref[r, :].astype(F32), la_b, l1_b)
        kf_ref[r, :] = 1.0 - jnp.exp(lgf)
        kb_ref[r, :] = 1.0 - jnp.exp(lgb)
        bf_ref[r, :] = _chunk_cumsum(lgf, row_in_chunk, False)
        bb_ref[r, :] = _chunk_cumsum(lgb, row_in_chunk, True)
        tot = jnp.minimum(jnp.sum(lgf.reshape(pb // C, C, HG_DK), axis=1),
                          jnp.sum(lgb.reshape(pb // C, C, HG_DK), axis=1))
        return jnp.minimum(lo, jnp.min(tot, axis=0, keepdims=True))

    lo = lax.fori_loop(0, seq // pb, pre, jnp.zeros((1, HG_DK), F32))
    bounded = jnp.min(lo) > -HG_SAFE_DECAY

    def scan(factorised):
        u = min(HG_UNROLL, n) if factorised else 1

        def body(g, carry):
            s_f, s_b = carry
            rows_f = [pl.ds(pl.multiple_of((g * u + j) * L, L), L) for j in range(u)]
            rows_b = [pl.ds(pl.multiple_of((n - 1 - g * u - j) * L, L), L) for j in range(u)]
            ins = []
            for j in range(u):
                ins.append((qs_ref[rows_f[j], :], kf_ref[rows_f[j], :], bf_ref[rows_f[j], :],
                            i_ref[rows_f[j], :], False, 0))
                ins.append((qs_ref[rows_b[j], :], kb_ref[rows_b[j], :], bb_ref[rows_b[j], :],
                            i_ref[rows_b[j], :], True, 1))
            outs, (s_f, s_b) = _hgrn_chunks(ins, (s_f, s_b), factorised)
            for j in range(u):
                of_ref[rows_f[j], :] = outs[2 * j]
                ob_ref[rows_b[j], :] = outs[2 * j + 1]
            return s_f, s_b

        s_f, s_b = lax.fori_loop(0, n // u, body, (s0_ref[0, 0, 0], s0_ref[0, 1, 0]))
        sT_ref[0, 0, 0] = s_f
        sT_ref[0, 1, 0] = s_b

    pl.when(bounded)(lambda: scan(True))
    pl.when(jnp.logical_not(bounded))(lambda: scan(False))

    o = of_ref[...] + ob_ref[...]
    o_ref[...] = (o * lax.rsqrt(jnp.mean(o * o, axis=-1, keepdims=True) + EPS) * ng_ref[...]).astype(o_ref.dtype)


def _hgrn(z, lbp, norm_g, s0_t, batch, seq):
    hb = BRANCH_W // HG_DK

    def col(name):
        base = ZOFF[name] // HG_DK
        return pl.BlockSpec((seq, HG_DK), lambda b, h: (b, base + h))

    st_spec = pl.BlockSpec((1, 2, 1, HG_DV, HG_DK), lambda b, h: (b, 0, h, 0, 0))
    return pl.pallas_call(
        functools.partial(_hgrn_kernel, seq=seq),
        grid=(batch, hb),
        in_specs=[col('a_q'), col('a_ff'), col('a_fb'), col('a_i'),
                  pl.BlockSpec((1, 4, HG_DK), lambda b, h: (h, 0, 0)),
                  pl.BlockSpec((1, HG_DV), lambda b, h: (0, 0)),
                  st_spec],
        out_specs=[pl.BlockSpec((seq, HG_DV), lambda b, h: (b, h)), st_spec],
        out_shape=[jax.ShapeDtypeStruct((batch * seq, BRANCH_W), z.dtype),
                   jax.ShapeDtypeStruct((batch, 2, HG_HEADS, HG_DV, HG_DK), F32)],
        scratch_shapes=[pltpu.VMEM((seq, HG_DK), F32)] * 5 + [pltpu.VMEM((seq, HG_DV), F32)] * 2,
        compiler_params=_cparams(("parallel", "parallel")),
        name="hgrn2",
    )(z, z, z, z, lbp, norm_g.reshape(1, HG_DV), s0_t)


def _flash_kernel(q_ref, k_ref, v_ref, o_ref, m_ref, l_ref, acc_ref, *, heads, dq, dv, scale):
    j = pl.program_id(2)

    @pl.when(j == 0)
    def _():
        m_ref[...] = jnp.full(m_ref.shape, NEG_BIG, F32)
        l_ref[...] = jnp.zeros(l_ref.shape, F32)
        acc_ref[...] = jnp.zeros(acc_ref.shape, F32)

    scores = [_bdot_nt(q_ref[:, h * dq:(h + 1) * dq], k_ref[:, h * dq:(h + 1) * dq]) * scale
              for h in range(heads)]
    probs = []
    for h, s in enumerate(scores):
        m_prev = m_ref[h][:, 0:1]
        m_new = jnp.maximum(m_prev, jnp.max(s, axis=1, keepdims=True))
        alpha = jnp.exp(m_prev - m_new)
        p = jnp.exp(s - m_new)
        l_new = alpha * l_ref[h][:, 0:1] + jnp.sum(p, axis=1, keepdims=True)
        m_ref[h] = jnp.broadcast_to(m_new, m_ref.shape[1:])
        l_ref[h] = jnp.broadcast_to(l_new, l_ref.shape[1:])
        probs.append((alpha, p.astype(BF16)))
    for h, (alpha, p) in enumerate(probs):
        hs = slice(h * dv, (h + 1) * dv)
        acc_ref[:, hs] = alpha * acc_ref[:, hs] + _bdot(p, v_ref[:, hs])

    @pl.when(j == pl.num_programs(2) - 1)
    def _():
        for h in range(heads):
            hs = slice(h * dv, (h + 1) * dv)
            o_ref[:, hs] = acc_ref[:, hs] / l_ref[h][:, 0:1]


def _flash(q, k, v, batch, tq_len, tk_len, heads, dq, dv, scale, qcol=0, kcol=0, vcol=0, tq=512, tk=256):
    tq = min(tq, tq_len)
    tk = min(tk, tk_len)
    nq, nk = tq_len // tq, tk_len // tk
    return pl.pallas_call(
        functools.partial(_flash_kernel, heads=heads, dq=dq, dv=dv, scale=scale),
        grid=(batch, nq, nk),
        in_specs=[pl.BlockSpec((tq, heads * dq), lambda b, i, j: (b * nq + i, qcol)),
                  pl.BlockSpec((tk, heads * dq), lambda b, i, j: (b * nk + j, kcol)),
                  pl.BlockSpec((tk, heads * dv), lambda b, i, j: (b * nk + j, vcol))],
        out_specs=pl.BlockSpec((tq, heads * dv), lambda b, i, j: (b * nq + i, 0)),
        out_shape=jax.ShapeDtypeStruct((batch * tq_len, heads * dv), F32),
        scratch_shapes=[pltpu.VMEM((heads, tq, LANE), F32), pltpu.VMEM((heads, tq, LANE), F32),
                        pltpu.VMEM((tq, heads * dv), F32)],
        compiler_params=_cparams(("parallel", "parallel", "arbitrary")),
        name="flash_attn",
    )(q, k, v)


def _mla_attn_kernel(q_ref, k_ref, v_ref, o_ref, *, chunks):
    tq = q_ref.shape[0]
    for h in range(MLA_HEADS):
        qs = slice(h * MLA_QW, (h + 1) * MLA_QW)
        vs = slice(h * MLA_VW, (h + 1) * MLA_VW)
        qh = q_ref[:, qs]
        m = jnp.full((tq, 1), NEG_BIG, F32)
        acc = jnp.zeros((tq, MLA_VW), F32)
        for c0, cn in chunks:
            s = _bdot_nt(qh, k_ref[c0:c0 + cn, qs])
            m_new = jnp.maximum(m, jnp.max(s, axis=1, keepdims=True))
            p = jnp.exp(s - m_new)
            acc = jnp.exp(m - m_new) * acc + _bdot(p, v_ref[c0:c0 + cn, vs])
            m = m_new
        o_ref[:, h * V_DIM:(h + 1) * V_DIM] = (acc[:, :V_DIM] / acc[:, V_DIM:2 * V_DIM]).astype(o_ref.dtype)


def _mla_attention(q, k, v, batch, tq_len, tk_len, out_dtype, tq=512):
    tq = min(tq, tq_len)
    nq = tq_len // tq
    chunks, c0 = [], 0
    while c0 < tk_len:
        cn = min(MLA_KV_CHUNK, tk_len - c0)
        chunks.append((c0, cn))
        c0 += cn
    return pl.pallas_call(
        functools.partial(_mla_attn_kernel, chunks=tuple(chunks)),
        grid=(batch, nq),
        in_specs=[pl.BlockSpec((tq, MLA_HEADS * MLA_QW), lambda b, i: (b * nq + i, 0)),
                  pl.BlockSpec((tk_len, MLA_HEADS * MLA_QW), lambda b, i: (b, 0)),
                  pl.BlockSpec((tk_len, MLA_HEADS * MLA_VW), lambda b, i: (b, 0))],
        out_specs=pl.BlockSpec((tq, MLA_HEADS * V_DIM), lambda b, i: (b * nq + i, 0)),
        out_shape=jax.ShapeDtypeStruct((batch * tq_len, MLA_HEADS * V_DIM), out_dtype),
        compiler_params=_cparams(("parallel", "arbitrary")),
        name="mla_attn",
    )(q, k, v)


def _na_kernel(q_ref, k_ref, v_ref, ck_ref, cv_ref, tab_ref, o_ref, *, rows):
    nloc = WIN_R * GRID_W
    pairs = [(j, h) for j in range(NA_ROWS_PER_STEP) for h in range(NA_HEADS)]
    k0s, tab0s = [], []
    for j in range(NA_ROWS_PER_STEP):
        r = pl.program_id(1) * NA_ROWS_PER_STEP + j
        rs = jnp.clip(r - WIN_R // 2, 0, rows - WIN_R)
        k0s.append(pl.multiple_of(rs * GRID_W, GRID_W))
        tab0s.append(rs - r + (WIN_R - 1))

    scores = []
    for j, h in pairs:
        hs = slice(h * NA_DH, (h + 1) * NA_DH)
        qh = q_ref[j * GRID_W:(j + 1) * GRID_W, hs]
        bias = jnp.concatenate([tab_ref[h, tab0s[j] + 2 * p] for p in range(WIN_R // 2)], axis=1)
        scores.append((_bdot_nt(qh, k_ref[pl.ds(k0s[j], nloc), hs]) + bias, _bdot_nt(qh, ck_ref[:, hs])))
    probs = []
    for s_loc, s_ctx in scores:
        m = jnp.maximum(jnp.max(s_loc, axis=1, keepdims=True), jnp.max(s_ctx, axis=1, keepdims=True))
        probs.append((jnp.exp(s_loc - m).astype(BF16), jnp.exp(s_ctx - m).astype(BF16)))
    outs = []
    for (j, h), (p_loc, p_ctx) in zip(pairs, probs):
        vs = slice(h * NA_VW, (h + 1) * NA_VW)
        o_aug = (jnp.dot(p_loc, v_ref[pl.ds(k0s[j], nloc), vs], preferred_element_type=F32)
                 + jnp.dot(p_ctx, cv_ref[:, vs], preferred_element_type=F32))
        outs.append(o_aug / pltpu.roll(o_aug, NA_DH, axis=1))
    lane = lax.broadcasted_iota(jnp.int32, (GRID_W, 2 * NA_DH), 1)
    for j in range(NA_ROWS_PER_STEP):
        for hp in range(NA_HEADS // 2):
            even, odd = outs[j * NA_HEADS + 2 * hp], outs[j * NA_HEADS + 2 * hp + 1]
            o_ref[j * GRID_W:(j + 1) * GRID_W, hp * 2 * NA_DH:(hp + 1) * 2 * NA_DH] = jnp.where(
                lane < NA_DH, even, pltpu.roll(odd, NA_DH, axis=1)).astype(o_ref.dtype)


def _na_bias_table(rpb):
    c = np.arange(GRID_W)
    cs = np.clip(c - WIN_C // 2, 0, GRID_W - WIN_C)
    kc = np.arange(GRID_W)
    valid = (kc[None, :] >= cs[:, None]) & (kc[None, :] < cs[:, None] + WIN_C)
    coff = np.clip(kc[None, :] - c[:, None] + (WIN_C - 1), 0, 2 * WIN_C - 2)
    onehot = (valid[:, :, None] & (coff[:, :, None] == np.arange(2 * WIN_C - 1))).astype(np.float32)
    tab = jnp.einsum('hdo,cko->hdck', rpb.astype(F32), onehot, precision=lax.Precision.HIGHEST)
    tab = tab + np.where(valid, 0.0, NEG_BIG).astype(np.float32)
    return jnp.concatenate([tab[:, :-1], tab[:, 1:]], axis=-1)


def _na_value_layout(v, ones_value):
    v3 = v.reshape(v.shape[:-1] + (NA_HEADS, NA_DH))
    pad = jnp.full(v3.shape[:-1] + (NA_VW - NA_DH,), ones_value, v.dtype)
    return jnp.concatenate([v3, pad], axis=-1).reshape(v.shape[:-1] + (NA_HEADS * NA_VW,))


def _na_attention(zn, col0, ck, cv, tab, batch, seq, past):
    rows = seq // GRID_W
    steps = rows // NA_ROWS_PER_STEP
    tq = NA_ROWS_PER_STEP * GRID_W
    vw = NA_HEADS * NA_VW
    qb = col0 // BRANCH_W
    vb = (col0 + 2 * BRANCH_W) // vw
    return pl.pallas_call(
        functools.partial(_na_kernel, rows=rows),
        grid=(batch, steps),
        in_specs=[pl.BlockSpec((tq, BRANCH_W), lambda b, r: (b * steps + r, qb)),
                  pl.BlockSpec((seq, BRANCH_W), lambda b, r: (b, qb + 1)),
                  pl.BlockSpec((seq, vw), lambda b, r: (b, vb)),
                  pl.BlockSpec((past, BRANCH_W), lambda b, r: (b, 0)),
                  pl.BlockSpec((past, vw), lambda b, r: (b, 0)),
                  pl.BlockSpec(tab.shape, lambda b, r: (0, 0, 0, 0))],
        out_specs=pl.BlockSpec((tq, BRANCH_W), lambda b, r: (b * steps + r, 0)),
        out_shape=jax.ShapeDtypeStruct((batch * seq, BRANCH_W), zn.dtype),
        compiler_params=_cparams(("parallel", "arbitrary")),
        name="na_attn",
    )(zn, zn, zn, ck, cv, tab)


def _rms(x, g):
    return x * lax.rsqrt(jnp.mean(x * x, axis=-1, keepdims=True) + EPS) * g


def _rotate_pairs(slab, cs):
    prod = slab * cs
    both = prod + pltpu.roll(prod, ROPE_DIM, axis=1)
    lane = lax.broadcasted_iota(jnp.int32, both.shape, 1)
    return jnp.where(lane < ROPE_DIM, both, 0.0)


def _mla_prep_kernel(*refs, has_q, norm_kv, in_place):
    if has_q:
        qd_ref, kvd_ref, kpe_ref, cs_ref, qg_ref, wq_ref, kg_ref, wk_ref, wv_ref, q_ref, k_ref, v_ref, ckv_ref = refs
    elif in_place:
        kvd_ref, kpe_ref, cs_ref, kg_ref, wk_ref, wv_ref, _, _, k_ref, v_ref, ckv_ref = refs
    else:
        kvd_ref, kpe_ref, cs_ref, kg_ref, wk_ref, wv_ref, k_ref, v_ref, ckv_ref = refs
    cs = cs_ref[...]
    if has_q:
        scale = (NOPE_DIM + ROPE_DIM) ** -0.5
        qf = _bdot(_rms(qd_ref[...].astype(F32), qg_ref[...]), wq_ref[...]) * scale
        for h in range(MLA_HEADS):
            base = h * MLA_QW
            q_ref[:, base:base + NOPE_DIM] = qf[:, base:base + NOPE_DIM].astype(q_ref.dtype)
            q_ref[:, base + NOPE_DIM:base + MLA_QW] = _rotate_pairs(
                qf[:, base + NOPE_DIM:base + MLA_QW], cs).astype(q_ref.dtype)
    kvd = kvd_ref[...].astype(F32)
    ckv = _rms(kvd, kg_ref[...]) if norm_kv else kvd
    ckv_ref[...] = ckv
    kn = _bdot(ckv, wk_ref[...])
    vv = _bdot(ckv, wv_ref[...]).astype(v_ref.dtype)
    kr = _rotate_pairs(kpe_ref[...].astype(F32), cs).astype(k_ref.dtype)
    ones = jnp.ones((vv.shape[0], MLA_VW - V_DIM), v_ref.dtype)
    for h in range(MLA_HEADS):
        base = h * MLA_QW
        k_ref[:, base:base + NOPE_DIM] = kn[:, h * NOPE_DIM:(h + 1) * NOPE_DIM].astype(k_ref.dtype)
        k_ref[:, base + NOPE_DIM:base + MLA_QW] = kr
        v_ref[:, h * MLA_VW:h * MLA_VW + V_DIM] = vv[:, h * V_DIM:(h + 1) * V_DIM]
        v_ref[:, h * MLA_VW + V_DIM:(h + 1) * MLA_VW] = ones


def _mla_prep(z, cs_tab, wts, seq, has_q, norm_kv, kvd=None, kpe=None, kv_rows=None, kv_row0=0, kv_into=None):
    qg, wq, kg, wk, wv = wts
    ntok = z.shape[0] if z is not None else kvd.shape[0]
    tm = min(MLA_PREP_ROWS, seq)
    per_b = seq // tm
    kv_rows = seq if kv_rows is None else kv_rows
    kv_per_b, kv_blk0 = kv_rows // tm, kv_row0 // tm
    kv_ntok = (ntok // seq) * kv_rows
    tok = lambda i: (i, 0)
    kv_tok = lambda i: ((i // per_b) * kv_per_b + kv_blk0 + i % per_b, 0)
    const = lambda i: (0, 0)
    in_specs, args = [], []
    if has_q:
        in_specs.append(pl.BlockSpec((tm, Q_RANK), lambda i: (i, ZOFF['c_qd'] // Q_RANK)))
        args.append(z)
    if z is not None:
        in_specs += [pl.BlockSpec((tm, KV_RANK), lambda i: (i, ZOFF['c_kvd'] // KV_RANK)),
                     pl.BlockSpec((tm, LANE), lambda i: (i, ZOFF['c_kpe'] // LANE))]
        args += [z, z]
    else:
        in_specs += [pl.BlockSpec((tm, KV_RANK), tok), pl.BlockSpec((tm, LANE), tok)]
        args += [kvd, kpe]
    in_specs.append(pl.BlockSpec((tm, LANE), lambda i: (i % per_b, 0)))
    args.append(cs_tab)
    if has_q:
        in_specs += [pl.BlockSpec((1, Q_RANK), const), pl.BlockSpec(wq.shape, const)]
        args += [qg.reshape(1, Q_RANK), wq]
    in_specs += [pl.BlockSpec((1, KV_RANK), const), pl.BlockSpec(wk.shape, const), pl.BlockSpec(wv.shape, const)]
    args += [kg.reshape(1, KV_RANK), wk, wv]
    aliases = {}
    if kv_into is not None:
        aliases = {len(args): 0, len(args) + 1: 1}
        in_specs += [pl.BlockSpec(memory_space=pl.ANY)] * 2
        args += list(kv_into)
    out_specs = [pl.BlockSpec((tm, MLA_HEADS * MLA_QW), kv_tok), pl.BlockSpec((tm, MLA_HEADS * MLA_VW), kv_tok),
                 pl.BlockSpec((tm, KV_RANK), tok)]
    out_shape = [jax.ShapeDtypeStruct((kv_ntok, MLA_HEADS * MLA_QW), BF16),
                 jax.ShapeDtypeStruct((kv_ntok, MLA_HEADS * MLA_VW), BF16),
                 jax.ShapeDtypeStruct((ntok, KV_RANK), F32)]
    if has_q:
        assert kv_into is None
        out_specs.insert(0, pl.BlockSpec((tm, MLA_HEADS * MLA_QW), tok))
        out_shape.insert(0, jax.ShapeDtypeStruct((ntok, MLA_HEADS * MLA_QW), BF16))
    return pl.pallas_call(
        functools.partial(_mla_prep_kernel, has_q=has_q, norm_kv=norm_kv, in_place=kv_into is not None),
        grid=(ntok // tm,),
        in_specs=in_specs, out_specs=out_specs, out_shape=out_shape,
        input_output_aliases=aliases,
        compiler_params=_cparams(("parallel",)),
        name="mla_prep",
    )(*args)


def _rope_table(seq):
    t = jnp.arange(seq)
    row = (t // GRID_W).astype(F32)
    col = (t % GRID_W).astype(F32)
    n_pair_axis = ROPE_DIM // 4
    inv = 1.0 / (ROPE_THETA ** (jnp.arange(n_pair_axis, dtype=F32) / n_pair_axis))
    ang = jnp.concatenate([row[:, None] * inv, col[:, None] * inv], axis=-1)
    cos, sin = jnp.cos(ang), jnp.sin(ang)
    return jnp.concatenate([cos, cos, -sin, sin], axis=-1)


def _identity_table(seq):
    return jnp.concatenate([jnp.ones((seq, ROPE_DIM), F32), jnp.zeros((seq, ROPE_DIM), F32)], axis=-1)


def _merge_kernel(x_ref, mod_ref, oa_ref, ob_ref, oc_ref, ga_ref, gb_ref, gc_ref, db_ref, dc_ref, dx_ref, dg_ref,
                  pc_ref, px_ref, nc_ref, nx_ref, wm_ref, bm_ref, wb_ref, wo_ref, cw_ref, lg_ref, lb_ref, y_ref,
                  *, per_b):
    tm = x_ref.shape[0]
    ti = pl.program_id(0) % per_b
    x = x_ref[...]
    mod = mod_ref[0]
    h = _modulate(x, mod).astype(BF16)
    gate = mod[:, 2 * D_MODEL:3 * D_MODEL]

    def f32(ref):
        return ref[...].astype(F32)

    u = f32(dc_ref) * f32(dx_ref)
    last = pc_ref.shape[0] - 1
    prev_row = jnp.where(ti > 0, f32(pc_ref)[last:last + 1, :] * f32(px_ref)[last:last + 1, :], 0.0)
    next_row = jnp.where(ti < per_b - 1, f32(nc_ref)[0:1, :] * f32(nx_ref)[0:1, :], 0.0)
    row = lax.broadcasted_iota(jnp.int32, u.shape, 0)
    u_prev = jnp.where(row == 0, prev_row, pltpu.roll(u, 1, axis=0))
    u_next = jnp.where(row == tm - 1, next_row, pltpu.roll(u, tm - 1, axis=0))
    conv = cw_ref[0:1, :] * u_prev + cw_ref[1:2, :] * u + cw_ref[2:3, :] * u_next
    out_d = f32(db_ref) * conv * _silu(f32(dg_ref))

    branches = (f32(oa_ref) * _silu(f32(ga_ref)), f32(ob_ref) * _silu(f32(gb_ref)),
                f32(oc_ref) * _silu(f32(gc_ref)), out_d)
    mixed = jnp.zeros((tm, D_MODEL), F32)
    for n in range(N_BRANCH):
        cols = slice(n * D_MODEL, (n + 1) * D_MODEL)
        mg = jnp.dot(h, wm_ref[:, cols], preferred_element_type=F32) + bm_ref[:, cols]
        mixed = mixed + jax.nn.sigmoid(mg) * _bdot(branches[n], wb_ref[n])
    out = _bdot(mixed, wo_ref[...])
    r = DEEPNORM_ALPHA * x + gate * out
    mu = jnp.mean(r, axis=-1, keepdims=True)
    rc = r - mu
    var = jnp.mean(rc * rc, axis=-1, keepdims=True)
    y_ref[...] = rc * lax.rsqrt(var + EPS) * lg_ref[...] + lb_ref[...]


def _merge(x2, mod, z, o_a, o_b, o_c, wts, seq):
    wm, bm, wb, wo, cw, ln_g, ln_b = wts
    ntok = x2.shape[0]
    tm = min(256, seq)
    per_b = seq // tm
    nb = mod.shape[0]
    halo_rows = BF16_SUBLANES if z.dtype == BF16 else SUBLANE
    hb = tm // halo_rows
    last_hb = ntok // halo_rows - 1
    mod_idx = (lambda i: (i // per_b, 0, 0)) if nb > 1 else (lambda i: (0, 0, 0))
    tok = lambda i: (i, 0)
    const2 = lambda i: (0, 0)

    def zcol(name):
        blk = ZOFF[name] // BRANCH_W
        return pl.BlockSpec((tm, BRANCH_W), lambda i: (i, blk))

    def halo(name, nxt):
        blk = ZOFF[name] // BRANCH_W
        if nxt:
            return pl.BlockSpec((halo_rows, BRANCH_W), lambda i: (jnp.minimum((i + 1) * hb, last_hb), blk))
        return pl.BlockSpec((halo_rows, BRANCH_W), lambda i: (jnp.maximum(i * hb - 1, 0), blk))

    br = pl.BlockSpec((tm, BRANCH_W), tok)
    in_specs = [pl.BlockSpec((tm, D_MODEL), tok), pl.BlockSpec((1, 1, 3 * D_MODEL), mod_idx),
                br, br, br, zcol('a_g'), zcol('b_g'), zcol('c_g'),
                zcol('d_b'), zcol('d_c'), zcol('d_x'), zcol('d_g'),
                halo('d_c', False), halo('d_x', False), halo('d_c', True), halo('d_x', True),
                pl.BlockSpec(wm.shape, const2), pl.BlockSpec(bm.shape, const2),
                pl.BlockSpec(wb.shape, lambda i: (0, 0, 0)), pl.BlockSpec(wo.shape, const2),
                pl.BlockSpec(cw.shape, const2), pl.BlockSpec((1, D_MODEL), const2),
                pl.BlockSpec((1, D_MODEL), const2)]
    return pl.pallas_call(
        functools.partial(_merge_kernel, per_b=per_b),
        grid=(ntok // tm,),
        in_specs=in_specs,
        out_specs=pl.BlockSpec((tm, D_MODEL), tok),
        out_shape=jax.ShapeDtypeStruct((ntok, D_MODEL), F32),
        compiler_params=_cparams(("parallel",)),
        name="merge_out",
    )(x2, mod, o_a, o_b, o_c, z, z, z, z, z, z, z, z, z, z, z, wm, bm, wb, wo, cw,
      ln_g.reshape(1, D_MODEL), ln_b.reshape(1, D_MODEL))


def _layer_weights(l, w_in, b_in, lb, hg_norm_g, na_rpb, mla_qnorm_g, mla_w_qb, mla_kvnorm_g, mla_w_kvb,
                   conv_w, w_branch, w_out, ln_g, ln_b):
    w_in_l = w_in[l].astype(BF16)
    b_in_l = b_in[l]
    wz = jnp.concatenate(_z_columns(w_in_l), axis=-1)
    bz = jnp.concatenate(_z_columns(b_in_l), axis=-1).reshape(1, ZW)
    wm = w_in_l[:, MERGE_OFF:]
    bm = b_in_l[MERGE_OFF:].reshape(1, N_BRANCH * D_MODEL)

    def ref_cols(a, name):
        return a[..., _REF_OFF[name]:_REF_OFF[name] + _REF_SIZE[name]]

    def latent_columns(a, ones_value):
        return jnp.concatenate(_z_columns(a, _Z_ORDER[:-3]) + [
            ref_cols(a, 'b_q') * NA_DH ** -0.5, ref_cols(a, 'b_k'),
            _na_value_layout(ref_cols(a, 'b_v'), ones_value)], axis=-1)

    wl = latent_columns(w_in_l, 0.0)
    bl = latent_columns(b_in_l, 1.0).reshape(1, -1)
    lb_f, lb_b = lb[0, l], lb[1, l]
    lbp = jnp.stack([jnp.log(lb_f), jnp.log1p(-lb_f), jnp.log(lb_b), jnp.log1p(-lb_b)], axis=0)
    lbp = jnp.transpose(lbp.reshape(4, HG_HEADS, HG_DK), (1, 0, 2))
    wq3 = mla_w_qb[l].reshape(Q_RANK, MLA_HEADS, NOPE_DIM + ROPE_DIM)
    half = NOPE_DIM + ROPE_DIM // 2
    wq = jnp.concatenate([wq3, wq3[..., half:], wq3[..., NOPE_DIM:half]], axis=-1)
    wq = wq.reshape(Q_RANK, MLA_HEADS * MLA_QW).astype(BF16)
    wkv3 = mla_w_kvb[l].reshape(KV_RANK, MLA_HEADS, NOPE_DIM + V_DIM)
    wk = wkv3[..., :NOPE_DIM].reshape(KV_RANK, MLA_HEADS * NOPE_DIM).astype(BF16)
    wv = wkv3[..., NOPE_DIM:].reshape(KV_RANK, MLA_HEADS * V_DIM).astype(BF16)
    return dict(
        wz=wz, bz=bz, wl=wl, bl=bl, lbp=lbp, hg_norm_g=hg_norm_g[l], na_tab=_na_bias_table(na_rpb[l]),
        mla=(mla_qnorm_g[l], wq, mla_kvnorm_g[l], wk, wv),
        merge=(wm, bm, w_branch[l].astype(BF16), w_out[l].astype(BF16), conv_w[l], ln_g[l], ln_b[l]))


def _context_layer(x2, mod, w, batch, seq):
    z = _in_proj(x2, mod, w['wz'], w['bz'], seq, F32)
    s0 = jnp.zeros((batch, 2, HG_HEADS, HG_DV, HG_DK), F32)
    o_a, s_t = _hgrn(z, w['lbp'], w['hg_norm_g'], s0, batch, seq)
    cq, ck, cv = (ZOFF[n] // BRANCH_W for n in ('b_q', 'b_k', 'b_v'))
    o_b = _flash(z, z, z, batch, seq, seq, NA_HEADS, NA_DH, NA_DH, NA_DH ** -0.5, cq, ck, cv)
    q, k, v, ckv = _mla_prep(z, _identity_table(seq), w['mla'], seq, True, True)
    o_c = _mla_attention(q, k, v, batch, seq, seq, F32)
    y = _merge(x2, mod, z, o_a, o_b, o_c, w['merge'], seq)

    def zslice(name, width):
        return z[:, ZOFF[name]:ZOFF[name] + width]

    cache = (jnp.swapaxes(s_t, -1, -2),
             zslice('b_k', BRANCH_W).reshape(batch, seq, NA_HEADS, NA_DH),
             zslice('b_v', BRANCH_W).reshape(batch, seq, NA_HEADS, NA_DH),
             ckv.reshape(batch, seq, KV_RANK),
             zslice('c_kpe', ROPE_DIM).reshape(batch, seq, ROPE_DIM))
    return y, cache


def _latent_layer(x2, mod, w, cache, batch, seq):
    s0, na_k, na_v, c_ckv, c_kpe = cache
    past = na_k.shape[1]
    z = _in_proj(x2, mod, w['wl'], w['bl'], seq, BF16)
    o_a, _ = _hgrn(z, w['lbp'], w['hg_norm_g'], jnp.swapaxes(s0, -1, -2), batch, seq)
    o_b = _na_attention(z, ZW_MAIN, na_k.reshape(batch * past, BRANCH_W).astype(BF16),
                        _na_value_layout(na_v.reshape(batch * past, BRANCH_W), 1.0).astype(BF16),
                        w['na_tab'], batch, seq, past)
    tk_len = seq + past
    q, k, v, _ = _mla_prep(z, _rope_table(seq), w['mla'], seq, True, True, kv_rows=tk_len)
    kpe_pad = jnp.pad(c_kpe.reshape(batch * past, ROPE_DIM), ((0, 0), (0, LANE - ROPE_DIM)))
    k, v, _ = _mla_prep(None, _identity_table(past), w['mla'], past, False, False,
                        kvd=c_ckv.reshape(batch * past, KV_RANK), kpe=kpe_pad,
                        kv_rows=tk_len, kv_row0=seq, kv_into=(k, v))
    o_c = _mla_attention(q, k, v, batch, seq, tk_len, BF16)
    return _merge(x2, mod, z, o_a, o_b, o_c, w['merge'], seq)


def kernel(x_prompt, x_sample, state_hgrn, cache_na_k, cache_na_v, cache_mla_ckv, cache_mla_kpe, c, c_ctx,
           w_ada, b_ada, w_in, b_in, hg_lb_logits, hg_norm_g, na_rpb, mla_qnorm_g, mla_w_qb, mla_kvnorm_g,
           mla_w_kvb, conv_w, w_branch, w_out, ln_g, ln_b):
    batch, seq, _ = x_prompt.shape
    dbatch, dseq, _ = x_sample.shape
    lb = jnp.cumsum(jax.nn.softmax(hg_lb_logits.astype(F32), axis=1), axis=1)
    lb = lb - lb[:, :1]
    n_cond = -(-(dbatch + 1) // SUBLANE) * SUBLANE
    cond = jnp.zeros((n_cond, D_MODEL), F32).at[:dbatch].set(c).at[dbatch].set(c_ctx)
    mod = _modulation(cond, w_ada, b_ada)

    y_p = x_prompt.reshape(batch * seq, D_MODEL)
    y_s = x_sample.reshape(dbatch * dseq, D_MODEL)
    caches = []
    for l in range(DEPTH):
        w = _layer_weights(l, w_in, b_in, lb, hg_norm_g, na_rpb, mla_qnorm_g, mla_w_qb, mla_kvnorm_g,
                           mla_w_kvb, conv_w, w_branch, w_out, ln_g, ln_b)
        mod_ctx = mod[l, dbatch:dbatch + 1].reshape(1, 1, 3 * D_MODEL)
        mod_lat = mod[l, :dbatch].reshape(dbatch, 1, 3 * D_MODEL)
        y_p, cache_l = _context_layer(y_p, mod_ctx, w, batch, seq)
        caches.append(cache_l)
        y_s = _latent_layer(y_s, mod_lat, w,
                            (state_hgrn[:, l], cache_na_k[:, l], cache_na_v[:, l],
                             cache_mla_ckv[:, l], cache_mla_kpe[:, l]), dbatch, dseq)
    outs = [jnp.stack([cl[i] for cl in caches], axis=1) for i in range(5)]
    return (y_p.reshape(batch, seq, D_MODEL), y_s.reshape(dbatch, dseq, D_MODEL), *outs)
```

```python
import functools

import numpy as np
import jax
import jax.numpy as jnp
from jax import lax
from jax.experimental import pallas as pl
from jax.experimental.pallas import tpu as pltpu

F32 = jnp.float32
BF16 = jnp.bfloat16

D_MODEL = 1024
DEPTH = 2
GRID_W = 64
N_BRANCH = 4
BRANCH_W = D_MODEL // 2
HG_DK = 128
HG_DV = 128
HG_HEADS = BRANCH_W // HG_DK
HG_CHUNK = 64
HG_SUB = 16
HG_NSUB = HG_CHUNK // HG_SUB
HG_SAFE_DECAY = 150.0
HG_PRE_ROWS = 512
HG_UNROLL = 8
NA_DH = 64
NA_HEADS = BRANCH_W // NA_DH
WIN_R = 8
WIN_C = 16
NA_VW = 128
NA_ROWS_PER_STEP = 8
NOPE_DIM = 128
ROPE_DIM = 64
V_DIM = 128
MLA_HEADS = BRANCH_W // V_DIM
MLA_QW = 256
MLA_VW = 256
MLA_KV_CHUNK = 1024
MLA_PREP_ROWS = 256
Q_RANK = 256
KV_RANK = 128
ROPE_THETA = 10000.0
CONV_W = 3
EPS = 1e-6
DEEPNORM_ALPHA = (2 * DEPTH) ** 0.25
NEG_BIG = -1e30

SPLIT_NAMES = ('a_q', 'a_ff', 'a_fb', 'a_i', 'a_g', 'b_q', 'b_k', 'b_v', 'b_g',
               'c_qd', 'c_kvd', 'c_kpe', 'c_g', 'd_b', 'd_c', 'd_x', 'd_g', 'merge')
SPLIT_SIZES = (BRANCH_W,) * 9 + (Q_RANK, KV_RANK, ROPE_DIM, BRANCH_W) + (BRANCH_W,) * 4 + (N_BRANCH * D_MODEL,)
_REF_OFF = dict(zip(SPLIT_NAMES, np.cumsum((0,) + SPLIT_SIZES[:-1]).tolist()))
_REF_SIZE = dict(zip(SPLIT_NAMES, SPLIT_SIZES))

_Z_ORDER = ('a_q', 'a_ff', 'a_fb', 'a_i', 'a_g', 'b_g', 'c_qd', 'c_kvd', 'c_kpe', 'c_kpe_sw', 'c_g',
            'd_b', 'd_c', 'd_x', 'd_g', 'b_q', 'b_k', 'b_v')


def _build_z_layout():
    off, pos = {}, 0
    for name in _Z_ORDER:
        off[name] = pos
        pos += _REF_SIZE['c_kpe' if name == 'c_kpe_sw' else name]
    return off, pos


ZOFF, ZW = _build_z_layout()


def _z_columns(w, order=_Z_ORDER):
    parts = []
    for name in order:
        if name == 'c_kpe_sw':
            base = _REF_OFF['c_kpe']
            parts += [w[..., base + ROPE_DIM // 2:base + ROPE_DIM], w[..., base:base + ROPE_DIM // 2]]
        else:
            parts.append(w[..., _REF_OFF[name]:_REF_OFF[name] + _REF_SIZE[name]])
    return parts


ZW_MAIN = ZOFF['b_q']
MERGE_OFF = _REF_OFF['merge']

LANE = 128
SUBLANE = 8
VMEM_LIMIT = 56 * 1024 * 1024
IN_PROJ_ROWS = 2048
IN_PROJ_COLS = 1024
BF16_SUBLANES = 16


def _cparams(sem):
    return pltpu.CompilerParams(dimension_semantics=sem, vmem_limit_bytes=VMEM_LIMIT)


def _bdot(a, b):
    return jnp.dot(a.astype(BF16), b.astype(BF16), preferred_element_type=F32)


def _bdot_nt(a, b):
    return lax.dot_general(a.astype(BF16), b.astype(BF16), (((1,), (1,)), ((), ())),
                           preferred_element_type=F32)


def _bdot_tn(a, b):
    return lax.dot_general(a.astype(BF16), b.astype(BF16), (((0,), (0,)), ((), ())),
                           preferred_element_type=F32)


def _silu(x):
    return x * jax.nn.sigmoid(x)


def _mod_kernel(c_ref, w_ref, b_ref, o_ref):
    o_ref[0] = _bdot(_silu(c_ref[...]), w_ref[0]) + b_ref[0]


def _modulation(cond, w_ada, b_ada):
    n = cond.shape[0]
    tn = D_MODEL
    return pl.pallas_call(
        _mod_kernel,
        grid=(DEPTH, 3 * D_MODEL // tn),
        in_specs=[pl.BlockSpec((n, D_MODEL), lambda l, j: (0, 0)),
                  pl.BlockSpec((1, D_MODEL, tn), lambda l, j: (l, 0, j)),
                  pl.BlockSpec((1, 1, tn), lambda l, j: (l, 0, j))],
        out_specs=pl.BlockSpec((1, n, tn), lambda l, j: (l, 0, j)),
        out_shape=jax.ShapeDtypeStruct((DEPTH, n, 3 * D_MODEL), F32),
        compiler_params=_cparams(("parallel", "parallel")),
        name="adaln_mod",
    )(cond, w_ada, b_ada.reshape(DEPTH, 1, 3 * D_MODEL))


def _modulate(x, mod_row):
    shift = mod_row[:, 0:D_MODEL]
    scale = mod_row[:, D_MODEL:2 * D_MODEL]
    return x * (1.0 + scale) + shift


def _in_proj_kernel(x_ref, mod_ref, w_ref, b_ref, o_ref, h_ref):
    @pl.when(pl.program_id(1) == 0)
    def _():
        h_ref[...] = _modulate(x_ref[...], mod_ref[0]).astype(BF16)

    o_ref[...] = (jnp.dot(h_ref[...], w_ref[...], preferred_element_type=F32) + b_ref[...]).astype(o_ref.dtype)


def _in_proj(x2, mod, w, b, seq, out_dtype):
    ntok = x2.shape[0]
    n = w.shape[1]
    nb = mod.shape[0]
    tm = min(IN_PROJ_ROWS, seq if nb > 1 else ntok)
    tn = IN_PROJ_COLS if n % IN_PROJ_COLS == 0 else IN_PROJ_COLS // 2
    per_b = seq // tm if nb > 1 else 1
    mod_idx = (lambda i, j: (i // per_b, 0, 0)) if nb > 1 else (lambda i, j: (0, 0, 0))
    return pl.pallas_call(
        _in_proj_kernel,
        grid=(ntok // tm, n // tn),
        in_specs=[pl.BlockSpec((tm, D_MODEL), lambda i, j: (i, 0)),
                  pl.BlockSpec((1, 1, 3 * D_MODEL), mod_idx),
                  pl.BlockSpec((D_MODEL, tn), lambda i, j: (0, j)),
                  pl.BlockSpec((1, tn), lambda i, j: (0, j))],
        out_specs=pl.BlockSpec((tm, tn), lambda i, j: (i, j)),
        out_shape=jax.ShapeDtypeStruct((ntok, n), out_dtype),
        scratch_shapes=[pltpu.VMEM((tm, D_MODEL), BF16)],
        compiler_params=_cparams(("parallel", "arbitrary")),
        name="in_proj",
    )(x2, mod, w, b)


def _hgrn_log_gate(fr, la, l1):
    ls = jnp.minimum(fr, 0.0) - jnp.log(1.0 + jnp.exp(-jnp.abs(fr)))
    c = l1 + ls
    return jnp.maximum(la, c) + jnp.log(1.0 + jnp.exp(-jnp.abs(la - c)))


def _chunk_cumsum(x, reverse):
    n = x.shape[0]
    tiles = HG_CHUNK // SUBLANE
    x4 = x.reshape(n // HG_CHUNK, tiles, SUBLANE, HG_DK)
    sub = lax.broadcasted_iota(jnp.int32, x4.shape, 2)
    k = 1
    while k < SUBLANE:
        if reverse:
            x4 = x4 + jnp.where(sub < SUBLANE - k, pltpu.roll(x4, SUBLANE - k, axis=2), 0.0)
        else:
            x4 = x4 + jnp.where(sub >= k, pltpu.roll(x4, k, axis=2), 0.0)
        k *= 2
    pieces = [None] * tiles
    carry = None
    for j in (range(tiles - 1, -1, -1) if reverse else range(tiles)):
        p = x4[:, j] if carry is None else x4[:, j] + carry
        pieces[j] = p
        edge = p[:, 0:1, :] if reverse else p[:, SUBLANE - 1:SUBLANE, :]
        carry = jnp.broadcast_to(edge, p.shape)
    return jnp.stack(pieces, axis=1).reshape(n, HG_DK)


def _hgrn_chunk_factors(q, kk, b, reverse):
    C, NS = HG_SUB, HG_NSUB

    pos = [(NS - 1 - i) if reverse else i for i in range(NS)]
    blk_at = {pos[i]: i for i in range(NS)}
    end_row = [(C * i) if reverse else (C * i + C - 1) for i in range(NS)]
    e_at = [b[end_row[blk_at[p]]:end_row[blk_at[p]] + 1, :] for p in range(NS)]
    zero_row = jnp.zeros_like(e_at[0])
    b_last = e_at[NS - 1]

    def rows(fn):
        return jnp.concatenate([jnp.broadcast_to(fn(pos[i]), (C, HG_DK)) for i in range(NS)], axis=0)

    bs = rows(lambda p: e_at[p - 1] if p > 0 else zero_row)
    be = rows(lambda p: e_at[p])
    qt = q * jnp.exp(b - bs)
    kh = kk * jnp.exp(be - b)
    qc = qt * jnp.exp(bs)
    kbar = kh * jnp.exp(b_last - be)
    q2 = qt * rows(lambda p: jnp.exp(e_at[p - 1] - e_at[p - 2]) if p >= 2 else zero_row)
    q3 = qt * rows(lambda p: jnp.exp(e_at[p - 1] - e_at[p - 3]) if p >= 3 else zero_row)
    return dict(qs=jnp.concatenate([qt, q2, q3], axis=0).astype(BF16), kh=kh.astype(BF16),
                qc=qc.astype(BF16), kbar=kbar.astype(BF16), state_decay=jnp.exp(b_last))


def _hgrn_chunk_scores(f, reverse):
    L, C = HG_CHUNK, HG_SUB
    ti = lax.broadcasted_iota(jnp.int32, (L, L), 0)
    si = lax.broadcasted_iota(jnp.int32, (L, L), 1)
    a_all = _bdot_nt(f['qs'], f['kh'])
    pt = ti // C
    ps = si // C
    gap = (ps - pt) if reverse else (pt - ps)
    a = (jnp.where(gap == 1, a_all[0:L], 0.0) + jnp.where(gap == 2, a_all[L:2 * L], 0.0)
         + jnp.where(gap == 3, a_all[2 * L:3 * L], 0.0))
    return a.astype(BF16)


def _hgrn_chunk_factors_centred(q, kk, b, reverse):
    L = HG_CHUNK
    b_first, b_last = (b[L - 1:L, :], b[0:1, :]) if reverse else (b[0:1, :], b[L - 1:L, :])
    d = b - 0.5 * (b_first + b_last)
    return dict(qm=(q * jnp.exp(d)).astype(BF16), km=(kk * jnp.exp(-d)).astype(BF16),
                qc=(q * jnp.exp(b)).astype(BF16), kbar=(kk * jnp.exp(b_last - b)).astype(BF16),
                state_decay=jnp.exp(b_last))


def _hgrn_chunk_scores_centred(f, reverse):
    L = HG_CHUNK
    ti = lax.broadcasted_iota(jnp.int32, (L, L), 0)
    si = lax.broadcasted_iota(jnp.int32, (L, L), 1)
    before = (si >= ti) if reverse else (si <= ti)
    return jnp.where(before, _bdot_nt(f['qm'], f['km']), 0.0).astype(BF16)


def _hgrn_pairwise_diag(q, kk, b, v, reverse):
    C, NS = HG_SUB, HG_NSUB
    s3 = lax.broadcasted_iota(jnp.int32, (C, C, HG_DK), 0)
    t3 = lax.broadcasted_iota(jnp.int32, (C, C, HG_DK), 1)
    keep = (t3 <= s3) if reverse else (t3 >= s3)
    ones = jnp.ones((HG_DK, HG_DV), BF16)
    diag = []
    for i in range(NS):
        sl = slice(C * i, C * (i + 1))
        bb, qb, kb, vb = b[sl], q[sl], kk[sl], v[sl].astype(F32)
        dec = jnp.exp(jnp.where(keep, bb[None, :, :] - bb[:, None, :], NEG_BIG))
        x = (qb[None, :, :] * kb[:, None, :] * dec).reshape(C * C, HG_DK)
        rep = jnp.dot(x.astype(BF16), ones, preferred_element_type=F32)
        diag.append(jnp.sum(rep.reshape(C, C, HG_DV) * vb[:, None, :], axis=0))
    return jnp.concatenate(diag, axis=0)


def _hgrn_chunks(ins, states, factorised):
    factors, scores = ((_hgrn_chunk_factors_centred, _hgrn_chunk_scores_centred) if factorised
                       else (_hgrn_chunk_factors, _hgrn_chunk_scores))
    fac = [factors(q, kk, b, rev) for q, kk, b, v, rev, _ in ins]
    att = [scores(f, c[4]) for f, c in zip(fac, ins)]
    intra = [_bdot(a, c[3]) for a, c in zip(att, ins)]
    upd = [_bdot_tn(c[3], f['kbar']) for f, c in zip(fac, ins)]
    if not factorised:
        intra = [o + _hgrn_pairwise_diag(q, kk, b, v, rev) for o, (q, kk, b, v, rev, _) in zip(intra, ins)]
    states = list(states)
    outs = []
    for f, c, o, u in zip(fac, ins, intra, upd):
        s_t = states[c[5]]
        outs.append(o + _bdot_nt(f['qc'], s_t))
        states[c[5]] = s_t * f['state_decay'] + u
    return outs, states


def _hgrn_kernel(q_ref, ff_ref, fb_ref, i_ref, lb_ref, ng_ref, s0_ref, o_ref, sT_ref,
                 qs_ref, kf_ref, kb_ref, bf_ref, bb_ref, of_ref, ob_ref, *, seq):
    L, C = HG_CHUNK, HG_SUB
    n = seq // L
    la_f, l1_f = lb_ref[0, 0:1, :], lb_ref[0, 1:2, :]
    la_b, l1_b = lb_ref[0, 2:3, :], lb_ref[0, 3:4, :]

    pb = min(HG_PRE_ROWS, seq)

    def pre(i, lo):
        r = pl.ds(pl.multiple_of(i * pb, pb), pb)
        qs_ref[r, :] = _silu(q_ref[r, :].astype(F32))
        lgf = _hgrn_log_gate(ff_ref[r, :].astype(F32), la_f, l1_f)
        lgb = _hgrn_log_gate(fb_ref[r, :].astype(F32), la_b, l1_b)
        kf_ref[r, :] = 1.0 - jnp.exp(lgf)
        kb_ref[r, :] = 1.0 - jnp.exp(lgb)
        bf = _chunk_cumsum(lgf, False)
        bb = _chunk_cumsum(lgb, True)
        bf_ref[r, :] = bf
        bb_ref[r, :] = bb
        return jnp.minimum(lo, jnp.min(jnp.minimum(bf, bb), axis=0, keepdims=True))

    lo = lax.fori_loop(0, seq // pb, pre, jnp.zeros((1, HG_DK), F32))
    bounded = jnp.min(lo) > -HG_SAFE_DECAY

    def scan(factorised):
        u = min(HG_UNROLL, n) if factorised else 1

        def body(g, carry):
            s_f, s_b = carry
            rows_f = [pl.ds(pl.multiple_of((g * u + j) * L, L), L) for j in range(u)]
            rows_b = [pl.ds(pl.multiple_of((n - 1 - g * u - j) * L, L), L) for j in range(u)]
            ins = []
            for j in range(u):
                ins.append((qs_ref[rows_f[j], :], kf_ref[rows_f[j], :], bf_ref[rows_f[j], :],
                            i_ref[rows_f[j], :], False, 0))
                ins.append((qs_ref[rows_b[j], :], kb_ref[rows_b[j], :], bb_ref[rows_b[j], :],
                            i_ref[rows_b[j], :], True, 1))
            outs, (s_f, s_b) = _hgrn_chunks(ins, (s_f, s_b), factorised)
            for j in range(u):
                of_ref[rows_f[j], :] = outs[2 * j]
                ob_ref[rows_b[j], :] = outs[2 * j + 1]
            return s_f, s_b

        s_f, s_b = lax.fori_loop(0, n // u, body, (s0_ref[0, 0, 0], s0_ref[0, 1, 0]))
        sT_ref[0, 0, 0] = s_f
        sT_ref[0, 1, 0] = s_b

    pl.when(bounded)(lambda: scan(True))
    pl.when(jnp.logical_not(bounded))(lambda: scan(False))

    o = of_ref[...] + ob_ref[...]
    o_ref[...] = (o * lax.rsqrt(jnp.mean(o * o, axis=-1, keepdims=True) + EPS) * ng_ref[...]).astype(o_ref.dtype)


def _hgrn(z, lbp, norm_g, s0_t, batch, seq):
    hb = BRANCH_W // HG_DK

    def col(name):
        base = ZOFF[name] // HG_DK
        return pl.BlockSpec((seq, HG_DK), lambda b, h: (b, base + h))

    st_spec = pl.BlockSpec((1, 2, 1, HG_DV, HG_DK), lambda b, h: (b, 0, h, 0, 0))
    return pl.pallas_call(
        functools.partial(_hgrn_kernel, seq=seq),
        grid=(batch, hb),
        in_specs=[col('a_q'), col('a_ff'), col('a_fb'), col('a_i'),
                  pl.BlockSpec((1, 4, HG_DK), lambda b, h: (h, 0, 0)),
                  pl.BlockSpec((1, HG_DV), lambda b, h: (0, 0)),
                  st_spec],
        out_specs=[pl.BlockSpec((seq, HG_DV), lambda b, h: (b, h)), st_spec],
        out_shape=[jax.ShapeDtypeStruct((batch * seq, BRANCH_W), z.dtype),
                   jax.ShapeDtypeStruct((batch, 2, HG_HEADS, HG_DV, HG_DK), F32)],
        scratch_shapes=[pltpu.VMEM((seq, HG_DK), F32)] * 5 + [pltpu.VMEM((seq, HG_DV), F32)] * 2,
        compiler_params=_cparams(("parallel", "parallel")),
        name="hgrn2",
    )(z, z, z, z, lbp, norm_g.reshape(1, HG_DV), s0_t)


def _flash_kernel(q_ref, k_ref, v_ref, o_ref, m_ref, l_ref, acc_ref, *, heads, dq, dv, scale):
    j = pl.program_id(2)

    @pl.when(j == 0)
    def _():
        m_ref[...] = jnp.full(m_ref.shape, NEG_BIG, F32)
        l_ref[...] = jnp.zeros(l_ref.shape, F32)
        acc_ref[...] = jnp.zeros(acc_ref.shape, F32)

    scores = [_bdot_nt(q_ref[:, h * dq:(h + 1) * dq], k_ref[:, h * dq:(h + 1) * dq]) * scale
              for h in range(heads)]
    probs = []
    for h, s in enumerate(scores):
        m_prev = m_ref[h][:, 0:1]
        m_new = jnp.maximum(m_prev, jnp.max(s, axis=1, keepdims=True))
        alpha = jnp.exp(m_prev - m_new)
        p = jnp.exp(s - m_new)
        l_new = alpha * l_ref[h][:, 0:1] + jnp.sum(p, axis=1, keepdims=True)
        m_ref[h] = jnp.broadcast_to(m_new, m_ref.shape[1:])
        l_ref[h] = jnp.broadcast_to(l_new, l_ref.shape[1:])
        probs.append((alpha, p.astype(BF16)))
    for h, (alpha, p) in enumerate(probs):
        hs = slice(h * dv, (h + 1) * dv)
        acc_ref[:, hs] = alpha * acc_ref[:, hs] + _bdot(p, v_ref[:, hs])

    @pl.when(j == pl.num_programs(2) - 1)
    def _():
        for h in range(heads):
            hs = slice(h * dv, (h + 1) * dv)
            o_ref[:, hs] = acc_ref[:, hs] / l_ref[h][:, 0:1]


def _flash(q, k, v, batch, tq_len, tk_len, heads, dq, dv, scale, qcol=0, kcol=0, vcol=0, tq=512, tk=256):
    tq = min(tq, tq_len)
    tk = min(tk, tk_len)
    nq, nk = tq_len // tq, tk_len // tk
    return pl.pallas_call(
        functools.partial(_flash_kernel, heads=heads, dq=dq, dv=dv, scale=scale),
        grid=(batch, nq, nk),
        in_specs=[pl.BlockSpec((tq, heads * dq), lambda b, i, j: (b * nq + i, qcol)),
                  pl.BlockSpec((tk, heads * dq), lambda b, i, j: (b * nk + j, kcol)),
                  pl.BlockSpec((tk, heads * dv), lambda b, i, j: (b * nk + j, vcol))],
        out_specs=pl.BlockSpec((tq, heads * dv), lambda b, i, j: (b * nq + i, 0)),
        out_shape=jax.ShapeDtypeStruct((batch * tq_len, heads * dv), F32),
        scratch_shapes=[pltpu.VMEM((heads, tq, LANE), F32), pltpu.VMEM((heads, tq, LANE), F32),
                        pltpu.VMEM((tq, heads * dv), F32)],
        compiler_params=_cparams(("parallel", "parallel", "arbitrary")),
        name="flash_attn",
    )(q, k, v)


def _mla_attn_kernel(q_ref, k_ref, v_ref, o_ref, *, chunks):
    tq = q_ref.shape[0]
    for h in range(MLA_HEADS):
        qs = slice(h * MLA_QW, (h + 1) * MLA_QW)
        vs = slice(h * MLA_VW, (h + 1) * MLA_VW)
        qh = q_ref[:, qs]
        m = jnp.full((tq, 1), NEG_BIG, F32)
        acc = jnp.zeros((tq, MLA_VW), F32)
        for c0, cn in chunks:
            s = _bdot_nt(qh, k_ref[c0:c0 + cn, qs])
            m_new = jnp.maximum(m, jnp.max(s, axis=1, keepdims=True))
            p = jnp.exp(s - m_new)
            acc = jnp.exp(m - m_new) * acc + _bdot(p, v_ref[c0:c0 + cn, vs])
            m = m_new
        o_ref[:, h * V_DIM:(h + 1) * V_DIM] = (acc[:, :V_DIM] / acc[:, V_DIM:2 * V_DIM]).astype(o_ref.dtype)


def _mla_attention(q, k, v, batch, tq_len, tk_len, out_dtype, tq=512):
    tq = min(tq, tq_len)
    nq = tq_len // tq
    chunks, c0 = [], 0
    while c0 < tk_len:
        cn = min(MLA_KV_CHUNK, tk_len - c0)
        chunks.append((c0, cn))
        c0 += cn
    return pl.pallas_call(
        functools.partial(_mla_attn_kernel, chunks=tuple(chunks)),
        grid=(batch, nq),
        in_specs=[pl.BlockSpec((tq, MLA_HEADS * MLA_QW), lambda b, i: (b * nq + i, 0)),
                  pl.BlockSpec((tk_len, MLA_HEADS * MLA_QW), lambda b, i: (b, 0)),
                  pl.BlockSpec((tk_len, MLA_HEADS * MLA_VW), lambda b, i: (b, 0))],
        out_specs=pl.BlockSpec((tq, MLA_HEADS * V_DIM), lambda b, i: (b * nq + i, 0)),
        out_shape=jax.ShapeDtypeStruct((batch * tq_len, MLA_HEADS * V_DIM), out_dtype),
        compiler_params=_cparams(("parallel", "arbitrary")),
        name="mla_attn",
    )(q, k, v)


def _na_kernel(q_ref, k_ref, v_ref, ck_ref, cv_ref, tab_ref, o_ref, *, rows):
    nloc = WIN_R * GRID_W
    pairs = [(j, h) for j in range(NA_ROWS_PER_STEP) for h in range(NA_HEADS)]
    k0s, tab0s = [], []
    for j in range(NA_ROWS_PER_STEP):
        r = pl.program_id(1) * NA_ROWS_PER_STEP + j
        rs = jnp.clip(r - WIN_R // 2, 0, rows - WIN_R)
        k0s.append(pl.multiple_of(rs * GRID_W, GRID_W))
        tab0s.append(rs - r + (WIN_R - 1))

    qlane = lax.broadcasted_iota(jnp.int32, (GRID_W, 2 * NA_DH), 1)
    scores = []
    for j, h in pairs:
        ts = slice((h // 2) * 2 * NA_DH, (h // 2 + 1) * 2 * NA_DH)
        qp = q_ref[j * GRID_W:(j + 1) * GRID_W, ts]
        qh = jnp.where((qlane >= NA_DH) if h % 2 else (qlane < NA_DH), qp, jnp.zeros_like(qp))
        bias = jnp.concatenate([tab_ref[h, tab0s[j] + 2 * p] for p in range(WIN_R // 2)], axis=1)
        scores.append((_bdot_nt(qh, k_ref[pl.ds(k0s[j], nloc), ts]) + bias, _bdot_nt(qh, ck_ref[:, ts])))
    probs = []
    for s_loc, s_ctx in scores:
        m = jnp.maximum(jnp.max(s_loc, axis=1, keepdims=True), jnp.max(s_ctx, axis=1, keepdims=True))
        probs.append((jnp.exp(s_loc - m).astype(BF16), jnp.exp(s_ctx - m).astype(BF16)))
    outs = []
    for (j, h), (p_loc, p_ctx) in zip(pairs, probs):
        vs = slice(h * NA_VW, (h + 1) * NA_VW)
        o_aug = (jnp.dot(p_loc, v_ref[pl.ds(k0s[j], nloc), vs], preferred_element_type=F32)
                 + jnp.dot(p_ctx, cv_ref[:, vs], preferred_element_type=F32))
        outs.append(o_aug / pltpu.roll(o_aug, NA_DH, axis=1))
    lane = lax.broadcasted_iota(jnp.int32, (GRID_W, 2 * NA_DH), 1)
    for j in range(NA_ROWS_PER_STEP):
        for hp in range(NA_HEADS // 2):
            even, odd = outs[j * NA_HEADS + 2 * hp], outs[j * NA_HEADS + 2 * hp + 1]
            o_ref[j * GRID_W:(j + 1) * GRID_W, hp * 2 * NA_DH:(hp + 1) * 2 * NA_DH] = jnp.where(
                lane < NA_DH, even, pltpu.roll(odd, NA_DH, axis=1)).astype(o_ref.dtype)


def _na_bias_table(rpb):
    c = np.arange(GRID_W)
    cs = np.clip(c - WIN_C // 2, 0, GRID_W - WIN_C)
    kc = np.arange(GRID_W)
    valid = (kc[None, :] >= cs[:, None]) & (kc[None, :] < cs[:, None] + WIN_C)
    coff = np.clip(kc[None, :] - c[:, None] + (WIN_C - 1), 0, 2 * WIN_C - 2)
    onehot = (valid[:, :, None] & (coff[:, :, None] == np.arange(2 * WIN_C - 1))).astype(np.float32)
    tab = jnp.einsum('hdo,cko->hdck', rpb.astype(F32), onehot, precision=lax.Precision.HIGHEST)
    tab = tab + np.where(valid, 0.0, NEG_BIG).astype(np.float32)
    return jnp.concatenate([tab[:, :-1], tab[:, 1:]], axis=-1)


def _na_value_layout(v, ones_value):
    v3 = v.reshape(v.shape[:-1] + (NA_HEADS, NA_DH))
    pad = jnp.full(v3.shape[:-1] + (NA_VW - NA_DH,), ones_value, v.dtype)
    return jnp.concatenate([v3, pad], axis=-1).reshape(v.shape[:-1] + (NA_HEADS * NA_VW,))


def _na_attention(zn, col0, ck, cv, tab, batch, seq, past):
    rows = seq // GRID_W
    steps = rows // NA_ROWS_PER_STEP
    tq = NA_ROWS_PER_STEP * GRID_W
    vw = NA_HEADS * NA_VW
    qb = col0 // BRANCH_W
    vb = (col0 + 2 * BRANCH_W) // vw
    return pl.pallas_call(
        functools.partial(_na_kernel, rows=rows),
        grid=(batch, steps),
        in_specs=[pl.BlockSpec((tq, BRANCH_W), lambda b, r: (b * steps + r, qb)),
                  pl.BlockSpec((seq, BRANCH_W), lambda b, r: (b, qb + 1)),
                  pl.BlockSpec((seq, vw), lambda b, r: (b, vb)),
                  pl.BlockSpec((past, BRANCH_W), lambda b, r: (b, 0)),
                  pl.BlockSpec((past, vw), lambda b, r: (b, 0)),
                  pl.BlockSpec(tab.shape, lambda b, r: (0, 0, 0, 0))],
        out_specs=pl.BlockSpec((tq, BRANCH_W), lambda b, r: (b * steps + r, 0)),
        out_shape=jax.ShapeDtypeStruct((batch * seq, BRANCH_W), zn.dtype),
        compiler_params=_cparams(("parallel", "arbitrary")),
        name="na_attn",
    )(zn, zn, zn, ck, cv, tab)


def _rms(x, g):
    return x * lax.rsqrt(jnp.mean(x * x, axis=-1, keepdims=True) + EPS) * g


def _rotate_pairs(slab, cs):
    prod = slab * cs
    both = prod + pltpu.roll(prod, ROPE_DIM, axis=1)
    lane = lax.broadcasted_iota(jnp.int32, both.shape, 1)
    return jnp.where(lane < ROPE_DIM, both, 0.0)


def _mla_prep_kernel(*refs, has_q, norm_kv, in_place):
    if has_q:
        qd_ref, kvd_ref, kpe_ref, cs_ref, qg_ref, wq_ref, kg_ref, wk_ref, wv_ref, q_ref, k_ref, v_ref, ckv_ref = refs
    elif in_place:
        kvd_ref, kpe_ref, cs_ref, kg_ref, wk_ref, wv_ref, _, _, k_ref, v_ref, ckv_ref = refs
    else:
        kvd_ref, kpe_ref, cs_ref, kg_ref, wk_ref, wv_ref, k_ref, v_ref, ckv_ref = refs
    cs = cs_ref[...]
    if has_q:
        scale = (NOPE_DIM + ROPE_DIM) ** -0.5
        qf = _bdot(_rms(qd_ref[...].astype(F32), qg_ref[...]), wq_ref[...]) * scale
        for h in range(MLA_HEADS):
            base = h * MLA_QW
            q_ref[:, base:base + NOPE_DIM] = qf[:, base:base + NOPE_DIM].astype(q_ref.dtype)
            q_ref[:, base + NOPE_DIM:base + MLA_QW] = _rotate_pairs(
                qf[:, base + NOPE_DIM:base + MLA_QW], cs).astype(q_ref.dtype)
    kvd = kvd_ref[...].astype(F32)
    ckv = _rms(kvd, kg_ref[...]) if norm_kv else kvd
    ckv_ref[...] = ckv
    kn = _bdot(ckv, wk_ref[...])
    vv = _bdot(ckv, wv_ref[...]).astype(v_ref.dtype)
    kr = _rotate_pairs(kpe_ref[...].astype(F32), cs).astype(k_ref.dtype)
    ones = jnp.ones((vv.shape[0], MLA_VW - V_DIM), v_ref.dtype)
    for h in range(MLA_HEADS):
        base = h * MLA_QW
        k_ref[:, base:base + NOPE_DIM] = kn[:, h * NOPE_DIM:(h + 1) * NOPE_DIM].astype(k_ref.dtype)
        k_ref[:, base + NOPE_DIM:base + MLA_QW] = kr
        v_ref[:, h * MLA_VW:h * MLA_VW + V_DIM] = vv[:, h * V_DIM:(h + 1) * V_DIM]
        v_ref[:, h * MLA_VW + V_DIM:(h + 1) * MLA_VW] = ones


def _mla_prep(z, cs_tab, wts, seq, has_q, norm_kv, kvd=None, kpe=None, kv_rows=None, kv_row0=0, kv_into=None):
    qg, wq, kg, wk, wv = wts
    ntok = z.shape[0] if z is not None else kvd.shape[0]
    tm = min(MLA_PREP_ROWS, seq)
    per_b = seq // tm
    kv_rows = seq if kv_rows is None else kv_rows
    kv_per_b, kv_blk0 = kv_rows // tm, kv_row0 // tm
    kv_ntok = (ntok // seq) * kv_rows
    tok = lambda i: (i, 0)
    kv_tok = lambda i: ((i // per_b) * kv_per_b + kv_blk0 + i % per_b, 0)
    const = lambda i: (0, 0)
    in_specs, args = [], []
    if has_q:
        in_specs.append(pl.BlockSpec((tm, Q_RANK), lambda i: (i, ZOFF['c_qd'] // Q_RANK)))
        args.append(z)
    if z is not None:
        in_specs += [pl.BlockSpec((tm, KV_RANK), lambda i: (i, ZOFF['c_kvd'] // KV_RANK)),
                     pl.BlockSpec((tm, LANE), lambda i: (i, ZOFF['c_kpe'] // LANE))]
        args += [z, z]
    else:
        in_specs += [pl.BlockSpec((tm, KV_RANK), tok), pl.BlockSpec((tm, LANE), tok)]
        args += [kvd, kpe]
    in_specs.append(pl.BlockSpec((tm, LANE), lambda i: (i % per_b, 0)))
    args.append(cs_tab)
    if has_q:
        in_specs += [pl.BlockSpec((1, Q_RANK), const), pl.BlockSpec(wq.shape, const)]
        args += [qg.reshape(1, Q_RANK), wq]
    in_specs += [pl.BlockSpec((1, KV_RANK), const), pl.BlockSpec(wk.shape, const), pl.BlockSpec(wv.shape, const)]
    args += [kg.reshape(1, KV_RANK), wk, wv]
    aliases = {}
    if kv_into is not None:
        aliases = {len(args): 0, len(args) + 1: 1}
        in_specs += [pl.BlockSpec(memory_space=pl.ANY)] * 2
        args += list(kv_into)
    out_specs = [pl.BlockSpec((tm, MLA_HEADS * MLA_QW), kv_tok), pl.BlockSpec((tm, MLA_HEADS * MLA_VW), kv_tok),
                 pl.BlockSpec((tm, KV_RANK), tok)]
    out_shape = [jax.ShapeDtypeStruct((kv_ntok, MLA_HEADS * MLA_QW), BF16),
                 jax.ShapeDtypeStruct((kv_ntok, MLA_HEADS * MLA_VW), BF16),
                 jax.ShapeDtypeStruct((ntok, KV_RANK), F32)]
    if has_q:
        assert kv_into is None
        out_specs.insert(0, pl.BlockSpec((tm, MLA_HEADS * MLA_QW), tok))
        out_shape.insert(0, jax.ShapeDtypeStruct((ntok, MLA_HEADS * MLA_QW), BF16))
    return pl.pallas_call(
        functools.partial(_mla_prep_kernel, has_q=has_q, norm_kv=norm_kv, in_place=kv_into is not None),
        grid=(ntok // tm,),
        in_specs=in_specs, out_specs=out_specs, out_shape=out_shape,
        input_output_aliases=aliases,
        compiler_params=_cparams(("parallel",)),
        name="mla_prep",
    )(*args)


def _rope_table(seq):
    t = jnp.arange(seq)
    row = (t // GRID_W).astype(F32)
    col = (t % GRID_W).astype(F32)
    n_pair_axis = ROPE_DIM // 4
    inv = 1.0 / (ROPE_THETA ** (jnp.arange(n_pair_axis, dtype=F32) / n_pair_axis))
    ang = jnp.concatenate([row[:, None] * inv, col[:, None] * inv], axis=-1)
    cos, sin = jnp.cos(ang), jnp.sin(ang)
    return jnp.concatenate([cos, cos, -sin, sin], axis=-1)


def _identity_table(seq):
    return jnp.concatenate([jnp.ones((seq, ROPE_DIM), F32), jnp.zeros((seq, ROPE_DIM), F32)], axis=-1)


def _merge_kernel(x_ref, mod_ref, oa_ref, ob_ref, oc_ref, ga_ref, gb_ref, gc_ref, db_ref, dc_ref, dx_ref, dg_ref,
                  pc_ref, px_ref, nc_ref, nx_ref, wm_ref, bm_ref, wb_ref, wo_ref, cw_ref, lg_ref, lb_ref, y_ref,
                  *, per_b):
    tm = x_ref.shape[0]
    ti = pl.program_id(0) % per_b
    x = x_ref[...]
    mod = mod_ref[0]
    h = _modulate(x, mod).astype(BF16)
    gate = mod[:, 2 * D_MODEL:3 * D_MODEL]

    def f32(ref):
        return ref[...].astype(F32)

    u = f32(dc_ref) * f32(dx_ref)
    last = pc_ref.shape[0] - 1
    prev_row = jnp.where(ti > 0, f32(pc_ref)[last:last + 1, :] * f32(px_ref)[last:last + 1, :], 0.0)
    next_row = jnp.where(ti < per_b - 1, f32(nc_ref)[0:1, :] * f32(nx_ref)[0:1, :], 0.0)
    row = lax.broadcasted_iota(jnp.int32, u.shape, 0)
    u_prev = jnp.where(row == 0, prev_row, pltpu.roll(u, 1, axis=0))
    u_next = jnp.where(row == tm - 1, next_row, pltpu.roll(u, tm - 1, axis=0))
    conv = cw_ref[0:1, :] * u_prev + cw_ref[1:2, :] * u + cw_ref[2:3, :] * u_next
    out_d = f32(db_ref) * conv * _silu(f32(dg_ref))

    branches = (f32(oa_ref) * _silu(f32(ga_ref)), f32(ob_ref) * _silu(f32(gb_ref)),
                f32(oc_ref) * _silu(f32(gc_ref)), out_d)
    mixed = jnp.zeros((tm, D_MODEL), F32)
    for n in range(N_BRANCH):
        cols = slice(n * D_MODEL, (n + 1) * D_MODEL)
        mg = jnp.dot(h, wm_ref[:, cols], preferred_element_type=F32) + bm_ref[:, cols]
        mixed = mixed + jax.nn.sigmoid(mg) * _bdot(branches[n], wb_ref[n])
    out = _bdot(mixed, wo_ref[...])
    r = DEEPNORM_ALPHA * x + gate * out
    mu = jnp.mean(r, axis=-1, keepdims=True)
    rc = r - mu
    var = jnp.mean(rc * rc, axis=-1, keepdims=True)
    y_ref[...] = rc * lax.rsqrt(var + EPS) * lg_ref[...] + lb_ref[...]


def _merge(x2, mod, z, o_a, o_b, o_c, wts, seq):
    wm, bm, wb, wo, cw, ln_g, ln_b = wts
    ntok = x2.shape[0]
    tm = min(256, seq)
    per_b = seq // tm
    nb = mod.shape[0]
    halo_rows = BF16_SUBLANES if z.dtype == BF16 else SUBLANE
    hb = tm // halo_rows
    last_hb = ntok // halo_rows - 1
    mod_idx = (lambda i: (i // per_b, 0, 0)) if nb > 1 else (lambda i: (0, 0, 0))
    tok = lambda i: (i, 0)
    const2 = lambda i: (0, 0)

    def zcol(name):
        blk = ZOFF[name] // BRANCH_W
        return pl.BlockSpec((tm, BRANCH_W), lambda i: (i, blk))

    def halo(name, nxt):
        blk = ZOFF[name] // BRANCH_W
        if nxt:
            return pl.BlockSpec((halo_rows, BRANCH_W), lambda i: (jnp.minimum((i + 1) * hb, last_hb), blk))
        return pl.BlockSpec((halo_rows, BRANCH_W), lambda i: (jnp.maximum(i * hb - 1, 0), blk))

    br = pl.BlockSpec((tm, BRANCH_W), tok)
    in_specs = [pl.BlockSpec((tm, D_MODEL), tok), pl.BlockSpec((1, 1, 3 * D_MODEL), mod_idx),
                br, br, br, zcol('a_g'), zcol('b_g'), zcol('c_g'),
                zcol('d_b'), zcol('d_c'), zcol('d_x'), zcol('d_g'),
                halo('d_c', False), halo('d_x', False), halo('d_c', True), halo('d_x', True),
                pl.BlockSpec(wm.shape, const2), pl.BlockSpec(bm.shape, const2),
                pl.BlockSpec(wb.shape, lambda i: (0, 0, 0)), pl.BlockSpec(wo.shape, const2),
                pl.BlockSpec(cw.shape, const2), pl.BlockSpec((1, D_MODEL), const2),
                pl.BlockSpec((1, D_MODEL), const2)]
    return pl.pallas_call(
        functools.partial(_merge_kernel, per_b=per_b),
        grid=(ntok // tm,),
        in_specs=in_specs,
        out_specs=pl.BlockSpec((tm, D_MODEL), tok),
        out_shape=jax.ShapeDtypeStruct((ntok, D_MODEL), F32),
        compiler_params=_cparams(("parallel",)),
        name="merge_out",
    )(x2, mod, o_a, o_b, o_c, z, z, z, z, z, z, z, z, z, z, z, wm, bm, wb, wo, cw,
      ln_g.reshape(1, D_MODEL), ln_b.reshape(1, D_MODEL))


def _layer_weights(l, w_in, b_in, lb, hg_norm_g, na_rpb, mla_qnorm_g, mla_w_qb, mla_kvnorm_g, mla_w_kvb,
                   conv_w, w_branch, w_out, ln_g, ln_b):
    w_in_l = w_in[l].astype(BF16)
    b_in_l = b_in[l]
    wz = jnp.concatenate(_z_columns(w_in_l), axis=-1)
    bz = jnp.concatenate(_z_columns(b_in_l), axis=-1).reshape(1, ZW)
    wm = w_in_l[:, MERGE_OFF:]
    bm = b_in_l[MERGE_OFF:].reshape(1, N_BRANCH * D_MODEL)

    def ref_cols(a, name):
        return a[..., _REF_OFF[name]:_REF_OFF[name] + _REF_SIZE[name]]

    def latent_columns(a, ones_value):
        return jnp.concatenate(_z_columns(a, _Z_ORDER[:-3]) + [
            ref_cols(a, 'b_q') * NA_DH ** -0.5, ref_cols(a, 'b_k'),
            _na_value_layout(ref_cols(a, 'b_v'), ones_value)], axis=-1)

    wl = latent_columns(w_in_l, 0.0)
    bl = latent_columns(b_in_l, 1.0).reshape(1, -1)
    lb_f, lb_b = lb[0, l], lb[1, l]
    lbp = jnp.stack([jnp.log(lb_f), jnp.log1p(-lb_f), jnp.log(lb_b), jnp.log1p(-lb_b)], axis=0)
    lbp = jnp.transpose(lbp.reshape(4, HG_HEADS, HG_DK), (1, 0, 2))
    wq3 = mla_w_qb[l].reshape(Q_RANK, MLA_HEADS, NOPE_DIM + ROPE_DIM)
    half = NOPE_DIM + ROPE_DIM // 2
    wq = jnp.concatenate([wq3, wq3[..., half:], wq3[..., NOPE_DIM:half]], axis=-1)
    wq = wq.reshape(Q_RANK, MLA_HEADS * MLA_QW).astype(BF16)
    wkv3 = mla_w_kvb[l].reshape(KV_RANK, MLA_HEADS, NOPE_DIM + V_DIM)
    wk = wkv3[..., :NOPE_DIM].reshape(KV_RANK, MLA_HEADS * NOPE_DIM).astype(BF16)
    wv = wkv3[..., NOPE_DIM:].reshape(KV_RANK, MLA_HEADS * V_DIM).astype(BF16)
    return dict(
        wz=wz, bz=bz, wl=wl, bl=bl, lbp=lbp, hg_norm_g=hg_norm_g[l], na_tab=_na_bias_table(na_rpb[l]),
        mla=(mla_qnorm_g[l], wq, mla_kvnorm_g[l], wk, wv),
        merge=(wm, bm, w_branch[l].astype(BF16), w_out[l].astype(BF16), conv_w[l], ln_g[l], ln_b[l]))


def _context_layer(x2, mod, w, batch, seq):
    z = _in_proj(x2, mod, w['wz'], w['bz'], seq, F32)
    s0 = jnp.zeros((batch, 2, HG_HEADS, HG_DV, HG_DK), F32)
    o_a, s_t = _hgrn(z, w['lbp'], w['hg_norm_g'], s0, batch, seq)
    cq, ck, cv = (ZOFF[n] // BRANCH_W for n in ('b_q', 'b_k', 'b_v'))
    o_b = _flash(z, z, z, batch, seq, seq, NA_HEADS, NA_DH, NA_DH, NA_DH ** -0.5, cq, ck, cv)
    q, k, v, ckv = _mla_prep(z, _identity_table(seq), w['mla'], seq, True, True)
    o_c = _mla_attention(q, k, v, batch, seq, seq, F32)
    y = _merge(x2, mod, z, o_a, o_b, o_c, w['merge'], seq)

    def zslice(name, width):
        return z[:, ZOFF[name]:ZOFF[name] + width]

    cache = (jnp.swapaxes(s_t, -1, -2),
             zslice('b_k', BRANCH_W).reshape(batch, seq, NA_HEADS, NA_DH),
             zslice('b_v', BRANCH_W).reshape(batch, seq, NA_HEADS, NA_DH),
             ckv.reshape(batch, seq, KV_RANK),
             zslice('c_kpe', ROPE_DIM).reshape(batch, seq, ROPE_DIM))
    return y, cache


def _latent_layer(x2, mod, w, cache, batch, seq):
    s0, na_k, na_v, c_ckv, c_kpe = cache
    past = na_k.shape[1]
    z = _in_proj(x2, mod, w['wl'], w['bl'], seq, BF16)
    o_a, _ = _hgrn(z, w['lbp'], w['hg_norm_g'], jnp.swapaxes(s0, -1, -2), batch, seq)
    o_b = _na_attention(z, ZW_MAIN, na_k.reshape(batch * past, BRANCH_W).astype(BF16),
                        _na_value_layout(na_v.reshape(batch * past, BRANCH_W), 1.0).astype(BF16),
                        w['na_tab'], batch, seq, past)
    tk_len = seq + past
    q, k, v, _ = _mla_prep(z, _rope_table(seq), w['mla'], seq, True, True, kv_rows=tk_len)
    kpe_pad = jnp.pad(c_kpe.reshape(batch * past, ROPE_DIM), ((0, 0), (0, LANE - ROPE_DIM)))
    k, v, _ = _mla_prep(None, _identity_table(past), w['mla'], past, False, False,
                        kvd=c_ckv.reshape(batch * past, KV_RANK), kpe=kpe_pad,
                        kv_rows=tk_len, kv_row0=seq, kv_into=(k, v))
    o_c = _mla_attention(q, k, v, batch, seq, tk_len, BF16)
    return _merge(x2, mod, z, o_a, o_b, o_c, w['merge'], seq)


def kernel(x_prompt, x_sample, state_hgrn, cache_na_k, cache_na_v, cache_mla_ckv, cache_mla_kpe, c, c_ctx,
           w_ada, b_ada, w_in, b_in, hg_lb_logits, hg_norm_g, na_rpb, mla_qnorm_g, mla_w_qb, mla_kvnorm_g,
           mla_w_kvb, conv_w, w_branch, w_out, ln_g, ln_b):
    batch, seq, _ = x_prompt.shape
    dbatch, dseq, _ = x_sample.shape
    lb = jnp.cumsum(jax.nn.softmax(hg_lb_logits.astype(F32), axis=1), axis=1)
    lb = lb - lb[:, :1]
    n_cond = -(-(dbatch + 1) // SUBLANE) * SUBLANE
    cond = jnp.zeros((n_cond, D_MODEL), F32).at[:dbatch].set(c).at[dbatch].set(c_ctx)
    mod = _modulation(cond, w_ada, b_ada)

    y_p = x_prompt.reshape(batch * seq, D_MODEL)
    y_s = x_sample.reshape(dbatch * dseq, D_MODEL)
    caches = []
    for l in range(DEPTH):
        w = _layer_weights(l, w_in, b_in, lb, hg_norm_g, na_rpb, mla_qnorm_g, mla_w_qb, mla_kvnorm_g,
                           mla_w_kvb, conv_w, w_branch, w_out, ln_g, ln_b)
        mod_ctx = mod[l, dbatch:dbatch + 1].reshape(1, 1, 3 * D_MODEL)
        mod_lat = mod[l, :dbatch].reshape(dbatch, 1, 3 * D_MODEL)
        y_p, cache_l = _context_layer(y_p, mod_ctx, w, batch, seq)
        caches.append(cache_l)
        y_s = _latent_layer(y_s, mod_lat, w,
                            (state_hgrn[:, l], cache_na_k[:, l], cache_na_v[:, l],
                             cache_mla_ckv[:, l], cache_mla_kpe[:, l]), dbatch, dseq)
    outs = [jnp.stack([cl[i] for cl in caches], axis=1) for i in range(5)]
    return (y_p.reshape(batch, seq, D_MODEL), y_s.reshape(dbatch, dseq, D_MODEL), *outs)
```

```python
import functools

import numpy as np
import jax
import jax.numpy as jnp
from jax import lax
from jax.experimental import pallas as pl
from jax.experimental.pallas import tpu as pltpu

F32 = jnp.float32
BF16 = jnp.bfloat16

D_MODEL = 1024
DEPTH = 2
GRID_W = 64
N_BRANCH = 4
BRANCH_W = D_MODEL // 2
HG_DK = 128
HG_DV = 128
HG_HEADS = BRANCH_W // HG_DK
HG_CHUNK = 64
HG_SUB = 16
HG_NSUB = HG_CHUNK // HG_SUB
HG_SAFE_DECAY = 150.0
HG_PRE_ROWS = 512
HG_UNROLL = 8
NA_DH = 64
NA_HEADS = BRANCH_W // NA_DH
WIN_R = 8
WIN_C = 16
NA_VW = 128
NA_ROWS_PER_STEP = 4
NOPE_DIM = 128
ROPE_DIM = 64
V_DIM = 128
MLA_HEADS = BRANCH_W // V_DIM
MLA_QW = 256
MLA_VW = 256
MLA_KV_CHUNK = 1024
MLA_PREP_ROWS = 256
Q_RANK = 256
KV_RANK = 128
ROPE_THETA = 10000.0
CONV_W = 3
EPS = 1e-6
DEEPNORM_ALPHA = (2 * DEPTH) ** 0.25
NEG_BIG = -1e30

SPLIT_NAMES = ('a_q', 'a_ff', 'a_fb', 'a_i', 'a_g', 'b_q', 'b_k', 'b_v', 'b_g',
               'c_qd', 'c_kvd', 'c_kpe', 'c_g', 'd_b', 'd_c', 'd_x', 'd_g', 'merge')
SPLIT_SIZES = (BRANCH_W,) * 9 + (Q_RANK, KV_RANK, ROPE_DIM, BRANCH_W) + (BRANCH_W,) * 4 + (N_BRANCH * D_MODEL,)
_REF_OFF = dict(zip(SPLIT_NAMES, np.cumsum((0,) + SPLIT_SIZES[:-1]).tolist()))
_REF_SIZE = dict(zip(SPLIT_NAMES, SPLIT_SIZES))

_Z_ORDER = ('a_q', 'a_ff', 'a_fb', 'a_i', 'a_g', 'b_g', 'c_qd', 'c_kvd', 'c_kpe', 'c_kpe_sw', 'c_g',
            'd_b', 'd_c', 'd_x', 'd_g', 'b_q', 'b_k', 'b_v')


def _build_z_layout():
    off, pos = {}, 0
    for name in _Z_ORDER:
        off[name] = pos
        pos += _REF_SIZE['c_kpe' if name == 'c_kpe_sw' else name]
    return off, pos


ZOFF, ZW = _build_z_layout()


def _z_columns(w, order=_Z_ORDER):
    parts = []
    for name in order:
        if name == 'c_kpe_sw':
            base = _REF_OFF['c_kpe']
            parts += [w[..., base + ROPE_DIM // 2:base + ROPE_DIM], w[..., base:base + ROPE_DIM // 2]]
        else:
            parts.append(w[..., _REF_OFF[name]:_REF_OFF[name] + _REF_SIZE[name]])
    return parts


ZW_MAIN = ZOFF['b_q']
MERGE_OFF = _REF_OFF['merge']

LANE = 128
SUBLANE = 8
VMEM_LIMIT = 56 * 1024 * 1024
IN_PROJ_ROWS = 2048
IN_PROJ_COLS = 1024
BF16_SUBLANES = 16
MERGE_ROWS = 512
MERGE_SUB_ROWS = 256


def _cparams(sem):
    return pltpu.CompilerParams(dimension_semantics=sem, vmem_limit_bytes=VMEM_LIMIT)


def _bdot(a, b):
    return jnp.dot(a.astype(BF16), b.astype(BF16), preferred_element_type=F32)


def _bdot_nt(a, b):
    return lax.dot_general(a.astype(BF16), b.astype(BF16), (((1,), (1,)), ((), ())),
                           preferred_element_type=F32)


def _bdot_tn(a, b):
    return lax.dot_general(a.astype(BF16), b.astype(BF16), (((0,), (0,)), ((), ())),
                           preferred_element_type=F32)


def _silu(x):
    return x * jax.nn.sigmoid(x)


def _mod_kernel(c_ref, w_ref, b_ref, o_ref):
    o_ref[0] = _bdot(_silu(c_ref[...]), w_ref[0]) + b_ref[0]


def _modulation(cond, w_ada, b_ada):
    n = cond.shape[0]
    tn = D_MODEL
    return pl.pallas_call(
        _mod_kernel,
        grid=(DEPTH, 3 * D_MODEL // tn),
        in_specs=[pl.BlockSpec((n, D_MODEL), lambda l, j: (0, 0)),
                  pl.BlockSpec((1, D_MODEL, tn), lambda l, j: (l, 0, j)),
                  pl.BlockSpec((1, 1, tn), lambda l, j: (l, 0, j))],
        out_specs=pl.BlockSpec((1, n, tn), lambda l, j: (l, 0, j)),
        out_shape=jax.ShapeDtypeStruct((DEPTH, n, 3 * D_MODEL), F32),
        compiler_params=_cparams(("parallel", "parallel")),
        name="adaln_mod",
    )(cond, w_ada, b_ada.reshape(DEPTH, 1, 3 * D_MODEL))


def _modulate(x, mod_row):
    shift = mod_row[:, 0:D_MODEL]
    scale = mod_row[:, D_MODEL:2 * D_MODEL]
    return x * (1.0 + scale) + shift


def _in_proj_kernel(x_ref, mod_ref, w_ref, b_ref, o_ref, h_ref):
    @pl.when(pl.program_id(1) == 0)
    def _():
        h_ref[...] = _modulate(x_ref[...], mod_ref[0]).astype(BF16)

    o_ref[...] = (jnp.dot(h_ref[...], w_ref[...], preferred_element_type=F32) + b_ref[...]).astype(o_ref.dtype)


def _in_proj(x2, mod, w, b, seq, out_dtype):
    ntok = x2.shape[0]
    n = w.shape[1]
    nb = mod.shape[0]
    tm = min(IN_PROJ_ROWS, seq if nb > 1 else ntok)
    tn = IN_PROJ_COLS if n % IN_PROJ_COLS == 0 else IN_PROJ_COLS // 2
    per_b = seq // tm if nb > 1 else 1
    mod_idx = (lambda i, j: (i // per_b, 0, 0)) if nb > 1 else (lambda i, j: (0, 0, 0))
    return pl.pallas_call(
        _in_proj_kernel,
        grid=(ntok // tm, n // tn),
        in_specs=[pl.BlockSpec((tm, D_MODEL), lambda i, j: (i, 0)),
                  pl.BlockSpec((1, 1, 3 * D_MODEL), mod_idx),
                  pl.BlockSpec((D_MODEL, tn), lambda i, j: (0, j)),
                  pl.BlockSpec((1, tn), lambda i, j: (0, j))],
        out_specs=pl.BlockSpec((tm, tn), lambda i, j: (i, j)),
        out_shape=jax.ShapeDtypeStruct((ntok, n), out_dtype),
        scratch_shapes=[pltpu.VMEM((tm, D_MODEL), BF16)],
        compiler_params=_cparams(("parallel", "arbitrary")),
        name="in_proj",
    )(x2, mod, w, b)


def _hgrn_log_gate(fr, la, l1):
    ls = jnp.minimum(fr, 0.0) - jnp.log(1.0 + jnp.exp(-jnp.abs(fr)))
    c = l1 + ls
    return jnp.maximum(la, c) + jnp.log(1.0 + jnp.exp(-jnp.abs(la - c)))


def _chunk_cumsum(x, reverse):
    n = x.shape[0]
    tiles = HG_CHUNK // SUBLANE
    x4 = x.reshape(n // HG_CHUNK, tiles, SUBLANE, HG_DK)
    sub = lax.broadcasted_iota(jnp.int32, x4.shape, 2)
    k = 1
    while k < SUBLANE:
        if reverse:
            x4 = x4 + jnp.where(sub < SUBLANE - k, pltpu.roll(x4, SUBLANE - k, axis=2), 0.0)
        else:
            x4 = x4 + jnp.where(sub >= k, pltpu.roll(x4, k, axis=2), 0.0)
        k *= 2
    pieces = [None] * tiles
    carry = None
    for j in (range(tiles - 1, -1, -1) if reverse else range(tiles)):
        p = x4[:, j] if carry is None else x4[:, j] + carry
        pieces[j] = p
        edge = p[:, 0:1, :] if reverse else p[:, SUBLANE - 1:SUBLANE, :]
        carry = jnp.broadcast_to(edge, p.shape)
    return jnp.stack(pieces, axis=1).reshape(n, HG_DK)


def _hgrn_chunk_factors(q, kk, b, reverse):
    C, NS = HG_SUB, HG_NSUB

    pos = [(NS - 1 - i) if reverse else i for i in range(NS)]
    blk_at = {pos[i]: i for i in range(NS)}
    end_row = [(C * i) if reverse else (C * i + C - 1) for i in range(NS)]
    e_at = [b[end_row[blk_at[p]]:end_row[blk_at[p]] + 1, :] for p in range(NS)]
    zero_row = jnp.zeros_like(e_at[0])
    b_last = e_at[NS - 1]

    def rows(fn):
        return jnp.concatenate([jnp.broadcast_to(fn(pos[i]), (C, HG_DK)) for i in range(NS)], axis=0)

    bs = rows(lambda p: e_at[p - 1] if p > 0 else zero_row)
    be = rows(lambda p: e_at[p])
    qt = q * jnp.exp(b - bs)
    kh = kk * jnp.exp(be - b)
    qc = qt * jnp.exp(bs)
    kbar = kh * jnp.exp(b_last - be)
    q2 = qt * rows(lambda p: jnp.exp(e_at[p - 1] - e_at[p - 2]) if p >= 2 else zero_row)
    q3 = qt * rows(lambda p: jnp.exp(e_at[p - 1] - e_at[p - 3]) if p >= 3 else zero_row)
    return dict(qs=jnp.concatenate([qt, q2, q3], axis=0).astype(BF16), kh=kh.astype(BF16),
                qc=qc.astype(BF16), kbar=kbar.astype(BF16), state_decay=jnp.exp(b_last))


def _hgrn_chunk_scores(f, reverse):
    L, C = HG_CHUNK, HG_SUB
    ti = lax.broadcasted_iota(jnp.int32, (L, L), 0)
    si = lax.broadcasted_iota(jnp.int32, (L, L), 1)
    a_all = _bdot_nt(f['qs'], f['kh'])
    pt = ti // C
    ps = si // C
    gap = (ps - pt) if reverse else (pt - ps)
    a = (jnp.where(gap == 1, a_all[0:L], 0.0) + jnp.where(gap == 2, a_all[L:2 * L], 0.0)
         + jnp.where(gap == 3, a_all[2 * L:3 * L], 0.0))
    return a.astype(BF16)


def _hgrn_chunk_factors_centred(q, kk, b, reverse):
    L = HG_CHUNK
    b_first, b_last = (b[L - 1:L, :], b[0:1, :]) if reverse else (b[0:1, :], b[L - 1:L, :])
    d = b - 0.5 * (b_first + b_last)
    return dict(qm=(q * jnp.exp(d)).astype(BF16), km=(kk * jnp.exp(-d)).astype(BF16),
                qc=(q * jnp.exp(b)).astype(BF16), kbar=(kk * jnp.exp(b_last - b)).astype(BF16),
                state_decay=jnp.exp(b_last))


def _hgrn_chunk_scores_centred(f, reverse):
    L = HG_CHUNK
    ti = lax.broadcasted_iota(jnp.int32, (L, L), 0)
    si = lax.broadcasted_iota(jnp.int32, (L, L), 1)
    before = (si >= ti) if reverse else (si <= ti)
    return jnp.where(before, _bdot_nt(f['qm'], f['km']), 0.0).astype(BF16)


def _hgrn_pairwise_diag(q, kk, b, v, reverse):
    C, NS = HG_SUB, HG_NSUB
    s3 = lax.broadcasted_iota(jnp.int32, (C, C, HG_DK), 0)
    t3 = lax.broadcasted_iota(jnp.int32, (C, C, HG_DK), 1)
    keep = (t3 <= s3) if reverse else (t3 >= s3)
    ones = jnp.ones((HG_DK, HG_DV), BF16)
    diag = []
    for i in range(NS):
        sl = slice(C * i, C * (i + 1))
        bb, qb, kb, vb = b[sl], q[sl], kk[sl], v[sl].astype(F32)
        dec = jnp.exp(jnp.where(keep, bb[None, :, :] - bb[:, None, :], NEG_BIG))
        x = (qb[None, :, :] * kb[:, None, :] * dec).reshape(C * C, HG_DK)
        rep = jnp.dot(x.astype(BF16), ones, preferred_element_type=F32)
        diag.append(jnp.sum(rep.reshape(C, C, HG_DV) * vb[:, None, :], axis=0))
    return jnp.concatenate(diag, axis=0)


def _hgrn_chunks(ins, states, factorised):
    factors, scores = ((_hgrn_chunk_factors_centred, _hgrn_chunk_scores_centred) if factorised
                       else (_hgrn_chunk_factors, _hgrn_chunk_scores))
    fac = [factors(q, kk, b, rev) for q, kk, b, v, rev, _ in ins]
    att = [scores(f, c[4]) for f, c in zip(fac, ins)]
    intra = [_bdot(a, c[3]) for a, c in zip(att, ins)]
    upd = [_bdot_tn(c[3], f['kbar']) for f, c in zip(fac, ins)]
    if not factorised:
        intra = [o + _hgrn_pairwise_diag(q, kk, b, v, rev) for o, (q, kk, b, v, rev, _) in zip(intra, ins)]
    states = list(states)
    outs = []
    for f, c, o, u in zip(fac, ins, intra, upd):
        s_t = states[c[5]]
        outs.append(o + _bdot_nt(f['qc'], s_t))
        states[c[5]] = s_t * f['state_decay'] + u
    return outs, states


def _hgrn_kernel(q_ref, ff_ref, fb_ref, i_ref, lb_ref, ng_ref, s0_ref, o_ref, sT_ref,
                 qs_ref, kf_ref, kb_ref, bf_ref, bb_ref, of_ref, ob_ref, *, seq):
    L, C = HG_CHUNK, HG_SUB
    n = seq // L
    la_f, l1_f = lb_ref[0, 0:1, :], lb_ref[0, 1:2, :]
    la_b, l1_b = lb_ref[0, 2:3, :], lb_ref[0, 3:4, :]

    pb = min(HG_PRE_ROWS, seq)

    def pre(i, lo):
        r = pl.ds(pl.multiple_of(i * pb, pb), pb)
        qs_ref[r, :] = _silu(q_ref[r, :].astype(F32))
        lgf = _hgrn_log_gate(ff_ref[r, :].astype(F32), la_f, l1_f)
        lgb = _hgrn_log_gate(fb_ref[r, :].astype(F32), la_b, l1_b)
        kf_ref[r, :] = 1.0 - jnp.exp(lgf)
        kb_ref[r, :] = 1.0 - jnp.exp(lgb)
        bf = _chunk_cumsum(lgf, False)
        bb = _chunk_cumsum(lgb, True)
        bf_ref[r, :] = bf
        bb_ref[r, :] = bb
        return jnp.minimum(lo, jnp.min(jnp.minimum(bf, bb), axis=0, keepdims=True))

    lo = lax.fori_loop(0, seq // pb, pre, jnp.zeros((1, HG_DK), F32))
    bounded = jnp.min(lo) > -HG_SAFE_DECAY

    def scan(factorised):
        u = min(HG_UNROLL, n) if factorised else 1

        def body(g, carry):
            s_f, s_b = carry
            rows_f = [pl.ds(pl.multiple_of((g * u + j) * L, L), L) for j in range(u)]
            rows_b = [pl.ds(pl.multiple_of((n - 1 - g * u - j) * L, L), L) for j in range(u)]
            ins = []
            for j in range(u):
                ins.append((qs_ref[rows_f[j], :], kf_ref[rows_f[j], :], bf_ref[rows_f[j], :],
                            i_ref[rows_f[j], :], False, 0))
                ins.append((qs_ref[rows_b[j], :], kb_ref[rows_b[j], :], bb_ref[rows_b[j], :],
                            i_ref[rows_b[j], :], True, 1))
            outs, (s_f, s_b) = _hgrn_chunks(ins, (s_f, s_b), factorised)
            for j in range(u):
                of_ref[rows_f[j], :] = outs[2 * j]
                ob_ref[rows_b[j], :] = outs[2 * j + 1]
            return s_f, s_b

        s_f, s_b = lax.fori_loop(0, n // u, body, (s0_ref[0, 0, 0], s0_ref[0, 1, 0]))
        sT_ref[0, 0, 0] = s_f
        sT_ref[0, 1, 0] = s_b

    pl.when(bounded)(lambda: scan(True))
    pl.when(jnp.logical_not(bounded))(lambda: scan(False))

    o = of_ref[...] + ob_ref[...]
    o_ref[...] = (o * lax.rsqrt(jnp.mean(o * o, axis=-1, keepdims=True) + EPS) * ng_ref[...]).astype(o_ref.dtype)


def _hgrn(z, lbp, norm_g, s0_t, batch, seq):
    hb = BRANCH_W // HG_DK

    def col(name):
        base = ZOFF[name] // HG_DK
        return pl.BlockSpec((seq, HG_DK), lambda b, h: (b, base + h))

    st_spec = pl.BlockSpec((1, 2, 1, HG_DV, HG_DK), lambda b, h: (b, 0, h, 0, 0))
    return pl.pallas_call(
        functools.partial(_hgrn_kernel, seq=seq),
        grid=(batch, hb),
        in_specs=[col('a_q'), col('a_ff'), col('a_fb'), col('a_i'),
                  pl.BlockSpec((1, 4, HG_DK), lambda b, h: (h, 0, 0)),
                  pl.BlockSpec((1, HG_DV), lambda b, h: (0, 0)),
                  st_spec],
        out_specs=[pl.BlockSpec((seq, HG_DV), lambda b, h: (b, h)), st_spec],
        out_shape=[jax.ShapeDtypeStruct((batch * seq, BRANCH_W), z.dtype),
                   jax.ShapeDtypeStruct((batch, 2, HG_HEADS, HG_DV, HG_DK), F32)],
        scratch_shapes=[pltpu.VMEM((seq, HG_DK), F32)] * 5 + [pltpu.VMEM((seq, HG_DV), F32)] * 2,
        compiler_params=_cparams(("parallel", "parallel")),
        name="hgrn2",
    )(z, z, z, z, lbp, norm_g.reshape(1, HG_DV), s0_t)


def _ctx_attn_kernel(q_ref, k_ref, v_ref, o_ref):
    seq = q_ref.shape[0]
    lane = lax.broadcasted_iota(jnp.int32, (seq, 2 * NA_DH), 1)
    scores = []
    for h in range(NA_HEADS):
        ts = slice((h // 2) * 2 * NA_DH, (h // 2 + 1) * 2 * NA_DH)
        qh = jnp.where((lane >= NA_DH) if h % 2 else (lane < NA_DH), q_ref[:, ts] * NA_DH ** -0.5, 0.0)
        scores.append(_bdot_nt(qh, k_ref[:, ts]))
    probs = []
    for s in scores:
        p = jnp.exp(s - jnp.max(s, axis=1, keepdims=True))
        probs.append((p.astype(BF16), jnp.sum(p, axis=1, keepdims=True)))
    for h, (p, denom) in enumerate(probs):
        hs = slice(h * NA_DH, (h + 1) * NA_DH)
        o_ref[:, hs] = _bdot(p, v_ref[:, hs]) / denom


def _ctx_attention(z, batch, seq):
    def col(name):
        blk = ZOFF[name] // BRANCH_W
        return pl.BlockSpec((seq, BRANCH_W), lambda b: (b, blk))

    return pl.pallas_call(
        _ctx_attn_kernel,
        grid=(batch,),
        in_specs=[col('b_q'), col('b_k'), col('b_v')],
        out_specs=pl.BlockSpec((seq, BRANCH_W), lambda b: (b, 0)),
        out_shape=jax.ShapeDtypeStruct((batch * seq, BRANCH_W), F32),
        compiler_params=_cparams(("parallel",)),
        name="ctx_attn",
    )(z, z, z)


def _mla_attn_kernel(q_ref, k_ref, v_ref, o_ref, *, chunks):
    tq = q_ref.shape[0]
    for h in range(MLA_HEADS):
        qs = slice(h * MLA_QW, (h + 1) * MLA_QW)
        vs = slice(h * MLA_VW, (h + 1) * MLA_VW)
        qh = q_ref[:, qs]
        m = jnp.full((tq, 1), NEG_BIG, F32)
        acc = jnp.zeros((tq, MLA_VW), F32)
        for c0, cn in chunks:
            s = _bdot_nt(qh, k_ref[c0:c0 + cn, qs])
            m_new = jnp.maximum(m, jnp.max(s, axis=1, keepdims=True))
            p = jnp.exp(s - m_new)
            acc = jnp.exp(m - m_new) * acc + _bdot(p, v_ref[c0:c0 + cn, vs])
            m = m_new
        o_ref[:, h * V_DIM:(h + 1) * V_DIM] = (acc[:, :V_DIM] / acc[:, V_DIM:2 * V_DIM]).astype(o_ref.dtype)


def _mla_attention(q, k, v, batch, tq_len, tk_len, out_dtype, tq=512):
    tq = min(tq, tq_len)
    nq = tq_len // tq
    chunks, c0 = [], 0
    while c0 < tk_len:
        cn = min(MLA_KV_CHUNK, tk_len - c0)
        chunks.append((c0, cn))
        c0 += cn
    return pl.pallas_call(
        functools.partial(_mla_attn_kernel, chunks=tuple(chunks)),
        grid=(batch, nq),
        in_specs=[pl.BlockSpec((tq, MLA_HEADS * MLA_QW), lambda b, i: (b * nq + i, 0)),
                  pl.BlockSpec((tk_len, MLA_HEADS * MLA_QW), lambda b, i: (b, 0)),
                  pl.BlockSpec((tk_len, MLA_HEADS * MLA_VW), lambda b, i: (b, 0))],
        out_specs=pl.BlockSpec((tq, MLA_HEADS * V_DIM), lambda b, i: (b * nq + i, 0)),
        out_shape=jax.ShapeDtypeStruct((batch * tq_len, MLA_HEADS * V_DIM), out_dtype),
        compiler_params=_cparams(("parallel", "arbitrary")),
        name="mla_attn",
    )(q, k, v)


def _na_kernel(q_ref, k_ref, v_ref, ck_ref, cv_ref, tab_ref, o_ref, *, rows):
    nloc = WIN_R * GRID_W
    pairs = [(j, h) for j in range(NA_ROWS_PER_STEP) for h in range(NA_HEADS)]
    k0s, tab0s = [], []
    for j in range(NA_ROWS_PER_STEP):
        r = pl.program_id(1) * NA_ROWS_PER_STEP + j
        rs = jnp.clip(r - WIN_R // 2, 0, rows - WIN_R)
        k0s.append(pl.multiple_of(rs * GRID_W, GRID_W))
        tab0s.append(rs - r + (WIN_R - 1))

    qlane = lax.broadcasted_iota(jnp.int32, (GRID_W, 2 * NA_DH), 1)
    scores = []
    for j, h in pairs:
        ts = slice((h // 2) * 2 * NA_DH, (h // 2 + 1) * 2 * NA_DH)
        qp = q_ref[j * GRID_W:(j + 1) * GRID_W, ts]
        qh = jnp.where((qlane >= NA_DH) if h % 2 else (qlane < NA_DH), qp, jnp.zeros_like(qp))
        bias = jnp.concatenate([tab_ref[h, tab0s[j] + 2 * p] for p in range(WIN_R // 2)], axis=1)
        scores.append((_bdot_nt(qh, k_ref[pl.ds(k0s[j], nloc), ts]) + bias, _bdot_nt(qh, ck_ref[:, ts])))
    probs = []
    for s_loc, s_ctx in scores:
        m = jnp.maximum(jnp.max(s_loc, axis=1, keepdims=True), jnp.max(s_ctx, axis=1, keepdims=True))
        probs.append((jnp.exp(s_loc - m).astype(BF16), jnp.exp(s_ctx - m).astype(BF16)))
    outs = []
    for (j, h), (p_loc, p_ctx) in zip(pairs, probs):
        vs = slice(h * NA_VW, (h + 1) * NA_VW)
        o_aug = (jnp.dot(p_loc, v_ref[pl.ds(k0s[j], nloc), vs], preferred_element_type=F32)
                 + jnp.dot(p_ctx, cv_ref[:, vs], preferred_element_type=F32))
        outs.append(o_aug / pltpu.roll(o_aug, NA_DH, axis=1))
    lane = lax.broadcasted_iota(jnp.int32, (GRID_W, 2 * NA_DH), 1)
    for j in range(NA_ROWS_PER_STEP):
        for hp in range(NA_HEADS // 2):
            even, odd = outs[j * NA_HEADS + 2 * hp], outs[j * NA_HEADS + 2 * hp + 1]
            o_ref[j * GRID_W:(j + 1) * GRID_W, hp * 2 * NA_DH:(hp + 1) * 2 * NA_DH] = jnp.where(
                lane < NA_DH, even, pltpu.roll(odd, NA_DH, axis=1)).astype(o_ref.dtype)


def _na_bias_table(rpb):
    c = np.arange(GRID_W)
    cs = np.clip(c - WIN_C // 2, 0, GRID_W - WIN_C)
    kc = np.arange(GRID_W)
    valid = (kc[None, :] >= cs[:, None]) & (kc[None, :] < cs[:, None] + WIN_C)
    coff = np.clip(kc[None, :] - c[:, None] + (WIN_C - 1), 0, 2 * WIN_C - 2)
    onehot = (valid[:, :, None] & (coff[:, :, None] == np.arange(2 * WIN_C - 1))).astype(np.float32)
    tab = jnp.einsum('hdo,cko->hdck', rpb.astype(F32), onehot, precision=lax.Precision.HIGHEST)
    tab = tab + np.where(valid, 0.0, NEG_BIG).astype(np.float32)
    return jnp.concatenate([tab[:, :-1], tab[:, 1:]], axis=-1)


def _na_value_layout(v, ones_value):
    v3 = v.reshape(v.shape[:-1] + (NA_HEADS, NA_DH))
    pad = jnp.full(v3.shape[:-1] + (NA_VW - NA_DH,), ones_value, v.dtype)
    return jnp.concatenate([v3, pad], axis=-1).reshape(v.shape[:-1] + (NA_HEADS * NA_VW,))


def _na_attention(zn, col0, ck, cv, tab, batch, seq, past):
    rows = seq // GRID_W
    steps = rows // NA_ROWS_PER_STEP
    tq = NA_ROWS_PER_STEP * GRID_W
    vw = NA_HEADS * NA_VW
    qb = col0 // BRANCH_W
    vb = (col0 + 2 * BRANCH_W) // vw
    return pl.pallas_call(
        functools.partial(_na_kernel, rows=rows),
        grid=(batch, steps),
        in_specs=[pl.BlockSpec((tq, BRANCH_W), lambda b, r: (b * steps + r, qb)),
                  pl.BlockSpec((seq, BRANCH_W), lambda b, r: (b, qb + 1)),
                  pl.BlockSpec((seq, vw), lambda b, r: (b, vb)),
                  pl.BlockSpec((past, BRANCH_W), lambda b, r: (b, 0)),
                  pl.BlockSpec((past, vw), lambda b, r: (b, 0)),
                  pl.BlockSpec(tab.shape, lambda b, r: (0, 0, 0, 0))],
        out_specs=pl.BlockSpec((tq, BRANCH_W), lambda b, r: (b * steps + r, 0)),
        out_shape=jax.ShapeDtypeStruct((batch * seq, BRANCH_W), zn.dtype),
        compiler_params=_cparams(("parallel", "arbitrary")),
        name="na_attn",
    )(zn, zn, zn, ck, cv, tab)


def _rms(x, g):
    return x * lax.rsqrt(jnp.mean(x * x, axis=-1, keepdims=True) + EPS) * g


def _rotate_pairs(slab, cs):
    prod = slab * cs
    both = prod + pltpu.roll(prod, ROPE_DIM, axis=1)
    lane = lax.broadcasted_iota(jnp.int32, both.shape, 1)
    return jnp.where(lane < ROPE_DIM, both, 0.0)


def _mla_prep_kernel(*refs, has_q, norm_kv, in_place):
    if has_q:
        qd_ref, kvd_ref, kpe_ref, cs_ref, qg_ref, wq_ref, kg_ref, wk_ref, wv_ref, q_ref, k_ref, v_ref, ckv_ref = refs
    elif in_place:
        kvd_ref, kpe_ref, cs_ref, kg_ref, wk_ref, wv_ref, _, _, k_ref, v_ref, ckv_ref = refs
    else:
        kvd_ref, kpe_ref, cs_ref, kg_ref, wk_ref, wv_ref, k_ref, v_ref, ckv_ref = refs
    cs = cs_ref[...]
    if has_q:
        scale = (NOPE_DIM + ROPE_DIM) ** -0.5
        qf = _bdot(_rms(qd_ref[...].astype(F32), qg_ref[...]), wq_ref[...]) * scale
        for h in range(MLA_HEADS):
            base = h * MLA_QW
            q_ref[:, base:base + NOPE_DIM] = qf[:, base:base + NOPE_DIM].astype(q_ref.dtype)
            q_ref[:, base + NOPE_DIM:base + MLA_QW] = _rotate_pairs(
                qf[:, base + NOPE_DIM:base + MLA_QW], cs).astype(q_ref.dtype)
    kvd = kvd_ref[...].astype(F32)
    ckv = _rms(kvd, kg_ref[...]) if norm_kv else kvd
    ckv_ref[...] = ckv
    kn = _bdot(ckv, wk_ref[...])
    vv = _bdot(ckv, wv_ref[...]).astype(v_ref.dtype)
    kr = _rotate_pairs(kpe_ref[...].astype(F32), cs).astype(k_ref.dtype)
    ones = jnp.ones((vv.shape[0], MLA_VW - V_DIM), v_ref.dtype)
    for h in range(MLA_HEADS):
        base = h * MLA_QW
        k_ref[:, base:base + NOPE_DIM] = kn[:, h * NOPE_DIM:(h + 1) * NOPE_DIM].astype(k_ref.dtype)
        k_ref[:, base + NOPE_DIM:base + MLA_QW] = kr
        v_ref[:, h * MLA_VW:h * MLA_VW + V_DIM] = vv[:, h * V_DIM:(h + 1) * V_DIM]
        v_ref[:, h * MLA_VW + V_DIM:(h + 1) * MLA_VW] = ones


def _mla_prep(z, cs_tab, wts, seq, has_q, norm_kv, kvd=None, kpe=None, kv_rows=None, kv_row0=0, kv_into=None):
    qg, wq, kg, wk, wv = wts
    ntok = z.shape[0] if z is not None else kvd.shape[0]
    tm = min(MLA_PREP_ROWS, seq)
    per_b = seq // tm
    kv_rows = seq if kv_rows is None else kv_rows
    kv_per_b, kv_blk0 = kv_rows // tm, kv_row0 // tm
    kv_ntok = (ntok // seq) * kv_rows
    tok = lambda i: (i, 0)
    kv_tok = lambda i: ((i // per_b) * kv_per_b + kv_blk0 + i % per_b, 0)
    const = lambda i: (0, 0)
    in_specs, args = [], []
    if has_q:
        in_specs.append(pl.BlockSpec((tm, Q_RANK), lambda i: (i, ZOFF['c_qd'] // Q_RANK)))
        args.append(z)
    if z is not None:
        in_specs += [pl.BlockSpec((tm, KV_RANK), lambda i: (i, ZOFF['c_kvd'] // KV_RANK)),
                     pl.BlockSpec((tm, LANE), lambda i: (i, ZOFF['c_kpe'] // LANE))]
        args += [z, z]
    else:
        in_specs += [pl.BlockSpec((tm, KV_RANK), tok), pl.BlockSpec((tm, LANE), tok)]
        args += [kvd, kpe]
    in_specs.append(pl.BlockSpec((tm, LANE), lambda i: (i % per_b, 0)))
    args.append(cs_tab)
    if has_q:
        in_specs += [pl.BlockSpec((1, Q_RANK), const), pl.BlockSpec(wq.shape, const)]
        args += [qg.reshape(1, Q_RANK), wq]
    in_specs += [pl.BlockSpec((1, KV_RANK), const), pl.BlockSpec(wk.shape, const), pl.BlockSpec(wv.shape, const)]
    args += [kg.reshape(1, KV_RANK), wk, wv]
    aliases = {}
    if kv_into is not None:
        aliases = {len(args): 0, len(args) + 1: 1}
        in_specs += [pl.BlockSpec(memory_space=pl.ANY)] * 2
        args += list(kv_into)
    out_specs = [pl.BlockSpec((tm, MLA_HEADS * MLA_QW), kv_tok), pl.BlockSpec((tm, MLA_HEADS * MLA_VW), kv_tok),
                 pl.BlockSpec((tm, KV_RANK), tok)]
    out_shape = [jax.ShapeDtypeStruct((kv_ntok, MLA_HEADS * MLA_QW), BF16),
                 jax.ShapeDtypeStruct((kv_ntok, MLA_HEADS * MLA_VW), BF16),
                 jax.ShapeDtypeStruct((ntok, KV_RANK), F32)]
    if has_q:
        assert kv_into is None
        out_specs.insert(0, pl.BlockSpec((tm, MLA_HEADS * MLA_QW), tok))
        out_shape.insert(0, jax.ShapeDtypeStruct((ntok, MLA_HEADS * MLA_QW), BF16))
    return pl.pallas_call(
        functools.partial(_mla_prep_kernel, has_q=has_q, norm_kv=norm_kv, in_place=kv_into is not None),
        grid=(ntok // tm,),
        in_specs=in_specs, out_specs=out_specs, out_shape=out_shape,
        input_output_aliases=aliases,
        compiler_params=_cparams(("parallel",)),
        name="mla_prep",
    )(*args)


def _rope_table(seq):
    t = jnp.arange(seq)
    row = (t // GRID_W).astype(F32)
    col = (t % GRID_W).astype(F32)
    n_pair_axis = ROPE_DIM // 4
    inv = 1.0 / (ROPE_THETA ** (jnp.arange(n_pair_axis, dtype=F32) / n_pair_axis))
    ang = jnp.concatenate([row[:, None] * inv, col[:, None] * inv], axis=-1)
    cos, sin = jnp.cos(ang), jnp.sin(ang)
    return jnp.concatenate([cos, cos, -sin, sin], axis=-1)


def _identity_table(seq):
    return jnp.concatenate([jnp.ones((seq, ROPE_DIM), F32), jnp.zeros((seq, ROPE_DIM), F32)], axis=-1)


def _merge_kernel(x_ref, mod_ref, oa_ref, ob_ref, oc_ref, ga_ref, gb_ref, gc_ref, db_ref, dc_ref, dx_ref, dg_ref,
                  pc_ref, px_ref, nc_ref, nx_ref, wm_ref, bm_ref, wb_ref, wo_ref, cw_ref, lg_ref, lb_ref, y_ref,
                  *, per_b):
    tm = x_ref.shape[0]
    ti = pl.program_id(0) % per_b
    x = x_ref[...]
    mod = mod_ref[0]
    h = _modulate(x, mod).astype(BF16)
    gate = mod[:, 2 * D_MODEL:3 * D_MODEL]

    def f32(ref):
        return ref[...].astype(F32)

    subs = [slice(r0, r0 + MERGE_SUB_ROWS) for r0 in range(0, tm, MERGE_SUB_ROWS)]
    merge_gates = [[jnp.dot(h[sl], wm_ref[:, n * D_MODEL:(n + 1) * D_MODEL], preferred_element_type=F32)
                    for n in range(N_BRANCH)] for sl in subs]

    u = f32(dc_ref) * f32(dx_ref)
    last = pc_ref.shape[0] - 1
    prev_row = jnp.where(ti > 0, f32(pc_ref)[last:last + 1, :] * f32(px_ref)[last:last + 1, :], 0.0)
    next_row = jnp.where(ti < per_b - 1, f32(nc_ref)[0:1, :] * f32(nx_ref)[0:1, :], 0.0)
    row = lax.broadcasted_iota(jnp.int32, u.shape, 0)
    u_prev = jnp.where(row == 0, prev_row, pltpu.roll(u, 1, axis=0))
    u_next = jnp.where(row == tm - 1, next_row, pltpu.roll(u, tm - 1, axis=0))
    conv = cw_ref[0:1, :] * u_prev + cw_ref[1:2, :] * u + cw_ref[2:3, :] * u_next
    out_d = f32(db_ref) * conv * _silu(f32(dg_ref))

    branches = [b.astype(BF16) for b in (f32(oa_ref) * _silu(f32(ga_ref)), f32(ob_ref) * _silu(f32(gb_ref)),
                                         f32(oc_ref) * _silu(f32(gc_ref)), out_d)]
    proj = [[jnp.dot(branches[n][sl], wb_ref[n], preferred_element_type=F32) for n in range(N_BRANCH)]
            for sl in subs]
    mixed = []
    for i in range(len(subs)):
        acc = jnp.zeros((MERGE_SUB_ROWS, D_MODEL), F32)
        for n in range(N_BRANCH):
            mg = merge_gates[i][n] + bm_ref[:, n * D_MODEL:(n + 1) * D_MODEL]
            acc = acc + jax.nn.sigmoid(mg) * proj[i][n]
        mixed.append(acc.astype(BF16))
    outs = [jnp.dot(m, wo_ref[...], preferred_element_type=F32) for m in mixed]
    for sl, out in zip(subs, outs):
        r = DEEPNORM_ALPHA * x[sl] + gate * out
        mu = jnp.mean(r, axis=-1, keepdims=True)
        rc = r - mu
        var = jnp.mean(rc * rc, axis=-1, keepdims=True)
        y_ref[sl, :] = rc * lax.rsqrt(var + EPS) * lg_ref[...] + lb_ref[...]


def _merge(x2, mod, z, o_a, o_b, o_c, wts, seq):
    wm, bm, wb, wo, cw, ln_g, ln_b = wts
    ntok = x2.shape[0]
    tm = min(MERGE_ROWS, seq)
    per_b = seq // tm
    nb = mod.shape[0]
    halo_rows = BF16_SUBLANES if z.dtype == BF16 else SUBLANE
    hb = tm // halo_rows
    last_hb = ntok // halo_rows - 1
    mod_idx = (lambda i: (i // per_b, 0, 0)) if nb > 1 else (lambda i: (0, 0, 0))
    tok = lambda i: (i, 0)
    const2 = lambda i: (0, 0)

    def zcol(name):
        blk = ZOFF[name] // BRANCH_W
        return pl.BlockSpec((tm, BRANCH_W), lambda i: (i, blk))

    def halo(name, nxt):
        blk = ZOFF[name] // BRANCH_W
        if nxt:
            return pl.BlockSpec((halo_rows, BRANCH_W), lambda i: (jnp.minimum((i + 1) * hb, last_hb), blk))
        return pl.BlockSpec((halo_rows, BRANCH_W), lambda i: (jnp.maximum(i * hb - 1, 0), blk))

    br = pl.BlockSpec((tm, BRANCH_W), tok)
    in_specs = [pl.BlockSpec((tm, D_MODEL), tok), pl.BlockSpec((1, 1, 3 * D_MODEL), mod_idx),
                br, br, br, zcol('a_g'), zcol('b_g'), zcol('c_g'),
                zcol('d_b'), zcol('d_c'), zcol('d_x'), zcol('d_g'),
                halo('d_c', False), halo('d_x', False), halo('d_c', True), halo('d_x', True),
                pl.BlockSpec(wm.shape, const2), pl.BlockSpec(bm.shape, const2),
                pl.BlockSpec(wb.shape, lambda i: (0, 0, 0)), pl.BlockSpec(wo.shape, const2),
                pl.BlockSpec(cw.shape, const2), pl.BlockSpec((1, D_MODEL), const2),
                pl.BlockSpec((1, D_MODEL), const2)]
    return pl.pallas_call(
        functools.partial(_merge_kernel, per_b=per_b),
        grid=(ntok // tm,),
        in_specs=in_specs,
        out_specs=pl.BlockSpec((tm, D_MODEL), tok),
        out_shape=jax.ShapeDtypeStruct((ntok, D_MODEL), F32),
        compiler_params=_cparams(("parallel",)),
        name="merge_out",
    )(x2, mod, o_a, o_b, o_c, z, z, z, z, z, z, z, z, z, z, z, wm, bm, wb, wo, cw,
      ln_g.reshape(1, D_MODEL), ln_b.reshape(1, D_MODEL))


def _layer_weights(l, w_in, b_in, lb, hg_norm_g, na_rpb, mla_qnorm_g, mla_w_qb, mla_kvnorm_g, mla_w_kvb,
                   conv_w, w_branch, w_out, ln_g, ln_b):
    w_in_l = w_in[l].astype(BF16)
    b_in_l = b_in[l]
    wz = jnp.concatenate(_z_columns(w_in_l), axis=-1)
    bz = jnp.concatenate(_z_columns(b_in_l), axis=-1).reshape(1, ZW)
    wm = w_in_l[:, MERGE_OFF:]
    bm = b_in_l[MERGE_OFF:].reshape(1, N_BRANCH * D_MODEL)

    def ref_cols(a, name):
        return a[..., _REF_OFF[name]:_REF_OFF[name] + _REF_SIZE[name]]

    def latent_columns(a, ones_value):
        return jnp.concatenate(_z_columns(a, _Z_ORDER[:-3]) + [
            ref_cols(a, 'b_q') * NA_DH ** -0.5, ref_cols(a, 'b_k'),
            _na_value_layout(ref_cols(a, 'b_v'), ones_value)], axis=-1)

    wl = latent_columns(w_in_l, 0.0)
    bl = latent_columns(b_in_l, 1.0).reshape(1, -1)
    lb_f, lb_b = lb[0, l], lb[1, l]
    lbp = jnp.stack([jnp.log(lb_f), jnp.log1p(-lb_f), jnp.log(lb_b), jnp.log1p(-lb_b)], axis=0)
    lbp = jnp.transpose(lbp.reshape(4, HG_HEADS, HG_DK), (1, 0, 2))
    wq3 = mla_w_qb[l].reshape(Q_RANK, MLA_HEADS, NOPE_DIM + ROPE_DIM)
    half = NOPE_DIM + ROPE_DIM // 2
    wq = jnp.concatenate([wq3, wq3[..., half:], wq3[..., NOPE_DIM:half]], axis=-1)
    wq = wq.reshape(Q_RANK, MLA_HEADS * MLA_QW).astype(BF16)
    wkv3 = mla_w_kvb[l].reshape(KV_RANK, MLA_HEADS, NOPE_DIM + V_DIM)
    wk = wkv3[..., :NOPE_DIM].reshape(KV_RANK, MLA_HEADS * NOPE_DIM).astype(BF16)
    wv = wkv3[..., NOPE_DIM:].reshape(KV_RANK, MLA_HEADS * V_DIM).astype(BF16)
    return dict(
        wz=wz, bz=bz, wl=wl, bl=bl, lbp=lbp, hg_norm_g=hg_norm_g[l], na_tab=_na_bias_table(na_rpb[l]),
        mla=(mla_qnorm_g[l], wq, mla_kvnorm_g[l], wk, wv),
        merge=(wm, bm, w_branch[l].astype(BF16), w_out[l].astype(BF16), conv_w[l], ln_g[l], ln_b[l]))


def _context_layer(x2, mod, w, batch, seq):
    z = _in_proj(x2, mod, w['wz'], w['bz'], seq, F32)
    s0 = jnp.zeros((batch, 2, HG_HEADS, HG_DV, HG_DK), F32)
    o_a, s_t = _hgrn(z, w['lbp'], w['hg_norm_g'], s0, batch, seq)
    o_b = _ctx_attention(z, batch, seq)
    q, k, v, ckv = _mla_prep(z, _identity_table(seq), w['mla'], seq, True, True)
    o_c = _mla_attention(q, k, v, batch, seq, seq, F32)
    y = _merge(x2, mod, z, o_a, o_b, o_c, w['merge'], seq)

    def zslice(name, width):
        return z[:, ZOFF[name]:ZOFF[name] + width]

    cache = (jnp.swapaxes(s_t, -1, -2),
             zslice('b_k', BRANCH_W).reshape(batch, seq, NA_HEADS, NA_DH),
             zslice('b_v', BRANCH_W).reshape(batch, seq, NA_HEADS, NA_DH),
             ckv.reshape(batch, seq, KV_RANK),
             zslice('c_kpe', ROPE_DIM).reshape(batch, seq, ROPE_DIM))
    return y, cache


def _latent_layer(x2, mod, w, cache, batch, seq):
    s0, na_k, na_v, c_ckv, c_kpe = cache
    past = na_k.shape[1]
    z = _in_proj(x2, mod, w['wl'], w['bl'], seq, BF16)
    o_a, _ = _hgrn(z, w['lbp'], w['hg_norm_g'], jnp.swapaxes(s0, -1, -2), batch, seq)
    o_b = _na_attention(z, ZW_MAIN, na_k.reshape(batch * past, BRANCH_W).astype(BF16),
                        _na_value_layout(na_v.reshape(batch * past, BRANCH_W), 1.0).astype(BF16),
                        w['na_tab'], batch, seq, past)
    tk_len = seq + past
    q, k, v, _ = _mla_prep(z, _rope_table(seq), w['mla'], seq, True, True, kv_rows=tk_len)
    kpe_pad = jnp.pad(c_kpe.reshape(batch * past, ROPE_DIM), ((0, 0), (0, LANE - ROPE_DIM)))
    k, v, _ = _mla_prep(None, _identity_table(past), w['mla'], past, False, False,
                        kvd=c_ckv.reshape(batch * past, KV_RANK), kpe=kpe_pad,
                        kv_rows=tk_len, kv_row0=seq, kv_into=(k, v))
    o_c = _mla_attention(q, k, v, batch, seq, tk_len, BF16)
    return _merge(x2, mod, z, o_a, o_b, o_c, w['merge'], seq)


def kernel(x_prompt, x_sample, state_hgrn, cache_na_k, cache_na_v, cache_mla_ckv, cache_mla_kpe, c, c_ctx,
           w_ada, b_ada, w_in, b_in, hg_lb_logits, hg_norm_g, na_rpb, mla_qnorm_g, mla_w_qb, mla_kvnorm_g,
           mla_w_kvb, conv_w, w_branch, w_out, ln_g, ln_b):
    batch, seq, _ = x_prompt.shape
    dbatch, dseq, _ = x_sample.shape
    lb = jnp.cumsum(jax.nn.softmax(hg_lb_logits.astype(F32), axis=1), axis=1)
    lb = lb - lb[:, :1]
    n_cond = -(-(dbatch + 1) // SUBLANE) * SUBLANE
    cond = jnp.zeros((n_cond, D_MODEL), F32).at[:dbatch].set(c).at[dbatch].set(c_ctx)
    mod = _modulation(cond, w_ada, b_ada)

    y_p = x_prompt.reshape(batch * seq, D_MODEL)
    y_s = x_sample.reshape(dbatch * dseq, D_MODEL)
    caches = []
    for l in range(DEPTH):
        w = _layer_weights(l, w_in, b_in, lb, hg_norm_g, na_rpb, mla_qnorm_g, mla_w_qb, mla_kvnorm_g,
                           mla_w_kvb, conv_w, w_branch, w_out, ln_g, ln_b)
        mod_ctx = mod[l, dbatch:dbatch + 1].reshape(1, 1, 3 * D_MODEL)
        mod_lat = mod[l, :dbatch].reshape(dbatch, 1, 3 * D_MODEL)
        y_p, cache_l = _context_layer(y_p, mod_ctx, w, batch, seq)
        caches.append(cache_l)
        y_s = _latent_layer(y_s, mod_lat, w,
                            (state_hgrn[:, l], cache_na_k[:, l], cache_na_v[:, l],
                             cache_mla_ckv[:, l], cache_mla_kpe[:, l]), dbatch, dseq)
    outs = [jnp.stack([cl[i] for cl in caches], axis=1) for i in range(5)]
    return (y_p.reshape(batch, seq, D_MODEL), y_s.reshape(dbatch, dseq, D_MODEL), *outs)
```

```python
import functools

import numpy as np
import jax
import jax.numpy as jnp
from jax import lax
from jax.experimental import pallas as pl
from jax.experimental.pallas import tpu as pltpu

F32 = jnp.float32
BF16 = jnp.bfloat16

D_MODEL = 1024
DEPTH = 2
GRID_W = 64
N_BRANCH = 4
BRANCH_W = D_MODEL // 2
HG_DK = 128
HG_DV = 128
HG_HEADS = BRANCH_W // HG_DK
HG_CHUNK = 64
HG_SUB = 16
HG_NSUB = HG_CHUNK // HG_SUB
HG_SAFE_DECAY = 150.0
HG_PRE_ROWS = 512
HG_UNROLL = 8
NA_DH = 64
NA_HEADS = BRANCH_W // NA_DH
WIN_R = 8
WIN_C = 16
NA_VW = 128
NA_ROWS_PER_STEP = 4
NOPE_DIM = 128
ROPE_DIM = 64
V_DIM = 128
MLA_HEADS = BRANCH_W // V_DIM
MLA_QW = 256
MLA_VW = 256
MLA_KV_CHUNK = 512
MLA_PREP_ROWS = 256
Q_RANK = 256
KV_RANK = 128
ROPE_THETA = 10000.0
CONV_W = 3
EPS = 1e-6
DEEPNORM_ALPHA = (2 * DEPTH) ** 0.25
NEG_BIG = -1e30

SPLIT_NAMES = ('a_q', 'a_ff', 'a_fb', 'a_i', 'a_g', 'b_q', 'b_k', 'b_v', 'b_g',
               'c_qd', 'c_kvd', 'c_kpe', 'c_g', 'd_b', 'd_c', 'd_x', 'd_g', 'merge')
SPLIT_SIZES = (BRANCH_W,) * 9 + (Q_RANK, KV_RANK, ROPE_DIM, BRANCH_W) + (BRANCH_W,) * 4 + (N_BRANCH * D_MODEL,)
_REF_OFF = dict(zip(SPLIT_NAMES, np.cumsum((0,) + SPLIT_SIZES[:-1]).tolist()))
_REF_SIZE = dict(zip(SPLIT_NAMES, SPLIT_SIZES))

_Z_ORDER = ('a_q', 'a_ff', 'a_fb', 'a_i', 'a_g', 'b_g', 'c_qd', 'c_kvd', 'c_kpe', 'c_kpe_sw', 'c_g',
            'd_b', 'd_c', 'd_x', 'd_g', 'b_q', 'b_k', 'b_v')


def _build_z_layout():
    off, pos = {}, 0
    for name in _Z_ORDER:
        off[name] = pos
        pos += _REF_SIZE['c_kpe' if name == 'c_kpe_sw' else name]
    return off, pos


ZOFF, ZW = _build_z_layout()


def _z_columns(w, order=_Z_ORDER):
    parts = []
    for name in order:
        if name == 'c_kpe_sw':
            base = _REF_OFF['c_kpe']
            parts += [w[..., base + ROPE_DIM // 2:base + ROPE_DIM], w[..., base:base + ROPE_DIM // 2]]
        else:
            parts.append(w[..., _REF_OFF[name]:_REF_OFF[name] + _REF_SIZE[name]])
    return parts


ZW_MAIN = ZOFF['b_q']
MERGE_OFF = _REF_OFF['merge']

LANE = 128
SUBLANE = 8
VMEM_LIMIT = 56 * 1024 * 1024
IN_PROJ_ROWS = 2048
IN_PROJ_COLS = 1024
BF16_SUBLANES = 16
MERGE_ROWS = 512
MERGE_SUB_ROWS = 512


def _cparams(sem):
    return pltpu.CompilerParams(dimension_semantics=sem, vmem_limit_bytes=VMEM_LIMIT)


def _bdot(a, b):
    return jnp.dot(a.astype(BF16), b.astype(BF16), preferred_element_type=F32)


def _bdot_nt(a, b):
    return lax.dot_general(a.astype(BF16), b.astype(BF16), (((1,), (1,)), ((), ())),
                           preferred_element_type=F32)


def _bdot_tn(a, b):
    return lax.dot_general(a.astype(BF16), b.astype(BF16), (((0,), (0,)), ((), ())),
                           preferred_element_type=F32)


def _silu(x):
    return x * jax.nn.sigmoid(x)


def _mod_kernel(c_ref, w_ref, b_ref, o_ref):
    o_ref[0] = _bdot(_silu(c_ref[...]), w_ref[0]) + b_ref[0]


def _modulation(cond, w_ada, b_ada):
    n = cond.shape[0]
    tn = D_MODEL
    return pl.pallas_call(
        _mod_kernel,
        grid=(DEPTH, 3 * D_MODEL // tn),
        in_specs=[pl.BlockSpec((n, D_MODEL), lambda l, j: (0, 0)),
                  pl.BlockSpec((1, D_MODEL, tn), lambda l, j: (l, 0, j)),
                  pl.BlockSpec((1, 1, tn), lambda l, j: (l, 0, j))],
        out_specs=pl.BlockSpec((1, n, tn), lambda l, j: (l, 0, j)),
        out_shape=jax.ShapeDtypeStruct((DEPTH, n, 3 * D_MODEL), F32),
        compiler_params=_cparams(("parallel", "parallel")),
        name="adaln_mod",
    )(cond, w_ada, b_ada.reshape(DEPTH, 1, 3 * D_MODEL))


def _modulate(x, mod_row):
    shift = mod_row[:, 0:D_MODEL]
    scale = mod_row[:, D_MODEL:2 * D_MODEL]
    return x * (1.0 + scale) + shift


def _in_proj_kernel(x_ref, mod_ref, w_ref, b_ref, o_ref, h_ref):
    @pl.when(pl.program_id(1) == 0)
    def _():
        h_ref[...] = _modulate(x_ref[...], mod_ref[0]).astype(BF16)

    o_ref[...] = (jnp.dot(h_ref[...], w_ref[...], preferred_element_type=F32) + b_ref[...]).astype(o_ref.dtype)


def _in_proj(x2, mod, w, b, seq, out_dtype):
    ntok = x2.shape[0]
    n = w.shape[1]
    nb = mod.shape[0]
    tm = min(IN_PROJ_ROWS, seq if nb > 1 else ntok)
    tn = IN_PROJ_COLS if n % IN_PROJ_COLS == 0 else IN_PROJ_COLS // 2
    per_b = seq // tm if nb > 1 else 1
    mod_idx = (lambda i, j: (i // per_b, 0, 0)) if nb > 1 else (lambda i, j: (0, 0, 0))
    return pl.pallas_call(
        _in_proj_kernel,
        grid=(ntok // tm, n // tn),
        in_specs=[pl.BlockSpec((tm, D_MODEL), lambda i, j: (i, 0)),
                  pl.BlockSpec((1, 1, 3 * D_MODEL), mod_idx),
                  pl.BlockSpec((D_MODEL, tn), lambda i, j: (0, j)),
                  pl.BlockSpec((1, tn), lambda i, j: (0, j))],
        out_specs=pl.BlockSpec((tm, tn), lambda i, j: (i, j)),
        out_shape=jax.ShapeDtypeStruct((ntok, n), out_dtype),
        scratch_shapes=[pltpu.VMEM((tm, D_MODEL), BF16)],
        compiler_params=_cparams(("parallel", "arbitrary")),
        name="in_proj",
    )(x2, mod, w, b)


def _hgrn_log_gate(fr, la, l1):
    ls = jnp.minimum(fr, 0.0) - jnp.log(1.0 + jnp.exp(-jnp.abs(fr)))
    c = l1 + ls
    return jnp.maximum(la, c) + jnp.log(1.0 + jnp.exp(-jnp.abs(la - c)))


def _chunk_cumsum(x, reverse):
    n = x.shape[0]
    tiles = HG_CHUNK // SUBLANE
    x4 = x.reshape(n // HG_CHUNK, tiles, SUBLANE, HG_DK)
    sub = lax.broadcasted_iota(jnp.int32, x4.shape, 2)
    k = 1
    while k < SUBLANE:
        if reverse:
            x4 = x4 + jnp.where(sub < SUBLANE - k, pltpu.roll(x4, SUBLANE - k, axis=2), 0.0)
        else:
            x4 = x4 + jnp.where(sub >= k, pltpu.roll(x4, k, axis=2), 0.0)
        k *= 2
    pieces = [None] * tiles
    carry = None
    for j in (range(tiles - 1, -1, -1) if reverse else range(tiles)):
        p = x4[:, j] if carry is None else x4[:, j] + carry
        pieces[j] = p
        edge = p[:, 0:1, :] if reverse else p[:, SUBLANE - 1:SUBLANE, :]
        carry = jnp.broadcast_to(edge, p.shape)
    return jnp.stack(pieces, axis=1).reshape(n, HG_DK)


def _hgrn_chunk_factors(q, kk, b, reverse):
    C, NS = HG_SUB, HG_NSUB

    pos = [(NS - 1 - i) if reverse else i for i in range(NS)]
    blk_at = {pos[i]: i for i in range(NS)}
    end_row = [(C * i) if reverse else (C * i + C - 1) for i in range(NS)]
    e_at = [b[end_row[blk_at[p]]:end_row[blk_at[p]] + 1, :] for p in range(NS)]
    zero_row = jnp.zeros_like(e_at[0])
    b_last = e_at[NS - 1]

    def rows(fn):
        return jnp.concatenate([jnp.broadcast_to(fn(pos[i]), (C, HG_DK)) for i in range(NS)], axis=0)

    bs = rows(lambda p: e_at[p - 1] if p > 0 else zero_row)
    be = rows(lambda p: e_at[p])
    qt = q * jnp.exp(b - bs)
    kh = kk * jnp.exp(be - b)
    qc = qt * jnp.exp(bs)
    kbar = kh * jnp.exp(b_last - be)
    q2 = qt * rows(lambda p: jnp.exp(e_at[p - 1] - e_at[p - 2]) if p >= 2 else zero_row)
    q3 = qt * rows(lambda p: jnp.exp(e_at[p - 1] - e_at[p - 3]) if p >= 3 else zero_row)
    return dict(qs=jnp.concatenate([qt, q2, q3], axis=0).astype(BF16), kh=kh.astype(BF16),
                qc=qc.astype(BF16), kbar=kbar.astype(BF16), state_decay=jnp.exp(b_last))


def _hgrn_chunk_scores(f, reverse):
    L, C = HG_CHUNK, HG_SUB
    ti = lax.broadcasted_iota(jnp.int32, (L, L), 0)
    si = lax.broadcasted_iota(jnp.int32, (L, L), 1)
    a_all = _bdot_nt(f['qs'], f['kh'])
    pt = ti // C
    ps = si // C
    gap = (ps - pt) if reverse else (pt - ps)
    a = (jnp.where(gap == 1, a_all[0:L], 0.0) + jnp.where(gap == 2, a_all[L:2 * L], 0.0)
         + jnp.where(gap == 3, a_all[2 * L:3 * L], 0.0))
    return a.astype(BF16)


def _hgrn_chunk_factors_centred(q, kk, b, reverse):
    L = HG_CHUNK
    b_first, b_last = (b[L - 1:L, :], b[0:1, :]) if reverse else (b[0:1, :], b[L - 1:L, :])
    d = b - 0.5 * (b_first + b_last)
    return dict(qm=(q * jnp.exp(d)).astype(BF16), km=(kk * jnp.exp(-d)).astype(BF16),
                qc=(q * jnp.exp(b)).astype(BF16), kbar=(kk * jnp.exp(b_last - b)).astype(BF16),
                state_decay=jnp.exp(b_last))


def _hgrn_chunk_scores_centred(f, reverse):
    L = HG_CHUNK
    ti = lax.broadcasted_iota(jnp.int32, (L, L), 0)
    si = lax.broadcasted_iota(jnp.int32, (L, L), 1)
    before = (si >= ti) if reverse else (si <= ti)
    return jnp.where(before, _bdot_nt(f['qm'], f['km']), 0.0).astype(BF16)


def _hgrn_pairwise_diag(q, kk, b, v, reverse):
    C, NS = HG_SUB, HG_NSUB
    s3 = lax.broadcasted_iota(jnp.int32, (C, C, HG_DK), 0)
    t3 = lax.broadcasted_iota(jnp.int32, (C, C, HG_DK), 1)
    keep = (t3 <= s3) if reverse else (t3 >= s3)
    ones = jnp.ones((HG_DK, HG_DV), BF16)
    diag = []
    for i in range(NS):
        sl = slice(C * i, C * (i + 1))
        bb, qb, kb, vb = b[sl], q[sl], kk[sl], v[sl].astype(F32)
        dec = jnp.exp(jnp.where(keep, bb[None, :, :] - bb[:, None, :], NEG_BIG))
        x = (qb[None, :, :] * kb[:, None, :] * dec).reshape(C * C, HG_DK)
        rep = jnp.dot(x.astype(BF16), ones, preferred_element_type=F32)
        diag.append(jnp.sum(rep.reshape(C, C, HG_DV) * vb[:, None, :], axis=0))
    return jnp.concatenate(diag, axis=0)


def _hgrn_chunks(ins, states, factorised):
    factors, scores = ((_hgrn_chunk_factors_centred, _hgrn_chunk_scores_centred) if factorised
                       else (_hgrn_chunk_factors, _hgrn_chunk_scores))
    fac = [factors(q, kk, b, rev) for q, kk, b, v, rev, _ in ins]
    att = [scores(f, c[4]) for f, c in zip(fac, ins)]
    intra = [_bdot(a, c[3]) for a, c in zip(att, ins)]
    upd = [_bdot_tn(c[3], f['kbar']) for f, c in zip(fac, ins)]
    if not factorised:
        intra = [o + _hgrn_pairwise_diag(q, kk, b, v, rev) for o, (q, kk, b, v, rev, _) in zip(intra, ins)]
    states = list(states)
    outs = []
    for f, c, o, u in zip(fac, ins, intra, upd):
        s_t = states[c[5]]
        outs.append(o + _bdot_nt(f['qc'], s_t))
        states[c[5]] = s_t * f['state_decay'] + u
    return outs, states


def _hgrn_kernel(q_ref, ff_ref, fb_ref, i_ref, lb_ref, ng_ref, s0_ref, o_ref, sT_ref,
                 qs_ref, kf_ref, kb_ref, bf_ref, bb_ref, of_ref, ob_ref, *, seq):
    L, C = HG_CHUNK, HG_SUB
    n = seq // L
    la_f, l1_f = lb_ref[0, 0:1, :], lb_ref[0, 1:2, :]
    la_b, l1_b = lb_ref[0, 2:3, :], lb_ref[0, 3:4, :]

    pb = min(HG_PRE_ROWS, seq)

    def pre(i, lo):
        r = pl.ds(pl.multiple_of(i * pb, pb), pb)
        qs_ref[r, :] = _silu(q_ref[r, :].astype(F32))
        lgf = _hgrn_log_gate(ff_ref[r, :].astype(F32), la_f, l1_f)
        lgb = _hgrn_log_gate(fb_ref[r, :].astype(F32), la_b, l1_b)
        kf_ref[r, :] = 1.0 - jnp.exp(lgf)
        kb_ref[r, :] = 1.0 - jnp.exp(lgb)
        bf = _chunk_cumsum(lgf, False)
        bb = _chunk_cumsum(lgb, True)
        bf_ref[r, :] = bf
        bb_ref[r, :] = bb
        return jnp.minimum(lo, jnp.min(jnp.minimum(bf, bb), axis=0, keepdims=True))

    lo = lax.fori_loop(0, seq // pb, pre, jnp.zeros((1, HG_DK), F32))
    bounded = jnp.min(lo) > -HG_SAFE_DECAY

    def scan(factorised):
        u = min(HG_UNROLL, n) if factorised else 1

        def body(g, carry):
            s_f, s_b = carry
            rows_f = [pl.ds(pl.multiple_of((g * u + j) * L, L), L) for j in range(u)]
            rows_b = [pl.ds(pl.multiple_of((n - 1 - g * u - j) * L, L), L) for j in range(u)]
            ins = []
            for j in range(u):
                ins.append((qs_ref[rows_f[j], :], kf_ref[rows_f[j], :], bf_ref[rows_f[j], :],
                            i_ref[rows_f[j], :], False, 0))
                ins.append((qs_ref[rows_b[j], :], kb_ref[rows_b[j], :], bb_ref[rows_b[j], :],
                            i_ref[rows_b[j], :], True, 1))
            outs, (s_f, s_b) = _hgrn_chunks(ins, (s_f, s_b), factorised)
            for j in range(u):
                of_ref[rows_f[j], :] = outs[2 * j]
                ob_ref[rows_b[j], :] = outs[2 * j + 1]
            return s_f, s_b

        s_f, s_b = lax.fori_loop(0, n // u, body, (s0_ref[0, 0, 0], s0_ref[0, 1, 0]))
        sT_ref[0, 0, 0] = s_f
        sT_ref[0, 1, 0] = s_b

    pl.when(bounded)(lambda: scan(True))
    pl.when(jnp.logical_not(bounded))(lambda: scan(False))

    o = of_ref[...] + ob_ref[...]
    o_ref[...] = (o * lax.rsqrt(jnp.mean(o * o, axis=-1, keepdims=True) + EPS) * ng_ref[...]).astype(o_ref.dtype)


def _hgrn(z, lbp, norm_g, s0_t, batch, seq):
    hb = BRANCH_W // HG_DK

    def col(name):
        base = ZOFF[name] // HG_DK
        return pl.BlockSpec((seq, HG_DK), lambda b, h: (b, base + h))

    st_spec = pl.BlockSpec((1, 2, 1, HG_DV, HG_DK), lambda b, h: (b, 0, h, 0, 0))
    return pl.pallas_call(
        functools.partial(_hgrn_kernel, seq=seq),
        grid=(batch, hb),
        in_specs=[col('a_q'), col('a_ff'), col('a_fb'), col('a_i'),
                  pl.BlockSpec((1, 4, HG_DK), lambda b, h: (h, 0, 0)),
                  pl.BlockSpec((1, HG_DV), lambda b, h: (0, 0)),
                  st_spec],
        out_specs=[pl.BlockSpec((seq, HG_DV), lambda b, h: (b, h)), st_spec],
        out_shape=[jax.ShapeDtypeStruct((batch * seq, BRANCH_W), z.dtype),
                   jax.ShapeDtypeStruct((batch, 2, HG_HEADS, HG_DV, HG_DK), F32)],
        scratch_shapes=[pltpu.VMEM((seq, HG_DK), F32)] * 5 + [pltpu.VMEM((seq, HG_DV), F32)] * 2,
        compiler_params=_cparams(("parallel", "parallel")),
        name="hgrn2",
    )(z, z, z, z, lbp, norm_g.reshape(1, HG_DV), s0_t)


def _ctx_attn_kernel(q_ref, k_ref, v_ref, o_ref):
    seq = q_ref.shape[0]
    lane = lax.broadcasted_iota(jnp.int32, (seq, 2 * NA_DH), 1)
    scores = []
    for h in range(NA_HEADS):
        ts = slice((h // 2) * 2 * NA_DH, (h // 2 + 1) * 2 * NA_DH)
        qh = jnp.where((lane >= NA_DH) if h % 2 else (lane < NA_DH), q_ref[:, ts] * NA_DH ** -0.5, 0.0)
        scores.append(_bdot_nt(qh, k_ref[:, ts]))
    probs = []
    for s in scores:
        p = jnp.exp(s - jnp.max(s, axis=1, keepdims=True))
        probs.append((p.astype(BF16), jnp.sum(p, axis=1, keepdims=True)))
    for h, (p, denom) in enumerate(probs):
        hs = slice(h * NA_DH, (h + 1) * NA_DH)
        o_ref[:, hs] = _bdot(p, v_ref[:, hs]) / denom


def _ctx_attention(z, batch, seq):
    def col(name):
        blk = ZOFF[name] // BRANCH_W
        return pl.BlockSpec((seq, BRANCH_W), lambda b: (b, blk))

    return pl.pallas_call(
        _ctx_attn_kernel,
        grid=(batch,),
        in_specs=[col('b_q'), col('b_k'), col('b_v')],
        out_specs=pl.BlockSpec((seq, BRANCH_W), lambda b: (b, 0)),
        out_shape=jax.ShapeDtypeStruct((batch * seq, BRANCH_W), F32),
        compiler_params=_cparams(("parallel",)),
        name="ctx_attn",
    )(z, z, z)


def _mla_attn_kernel(q_ref, k_ref, v_ref, o_ref, *, chunks):
    tq = q_ref.shape[0]
    for h in range(MLA_HEADS):
        qs = slice(h * MLA_QW, (h + 1) * MLA_QW)
        vs = slice(h * MLA_VW, (h + 1) * MLA_VW)
        qh = q_ref[:, qs]
        m = jnp.full((tq, 1), NEG_BIG, F32)
        acc = jnp.zeros((tq, MLA_VW), F32)
        for c0, cn in chunks:
            s = _bdot_nt(qh, k_ref[c0:c0 + cn, qs])
            m_new = jnp.maximum(m, jnp.max(s, axis=1, keepdims=True))
            p = jnp.exp(s - m_new)
            acc = jnp.exp(m - m_new) * acc + _bdot(p, v_ref[c0:c0 + cn, vs])
            m = m_new
        o_ref[:, h * V_DIM:(h + 1) * V_DIM] = (acc[:, :V_DIM] / acc[:, V_DIM:2 * V_DIM]).astype(o_ref.dtype)


def _mla_attention(q, k, v, batch, tq_len, tk_len, out_dtype, tq=1024):
    tq = min(tq, tq_len)
    nq = tq_len // tq
    chunks, c0 = [], 0
    while c0 < tk_len:
        cn = min(MLA_KV_CHUNK, tk_len - c0)
        chunks.append((c0, cn))
        c0 += cn
    return pl.pallas_call(
        functools.partial(_mla_attn_kernel, chunks=tuple(chunks)),
        grid=(batch, nq),
        in_specs=[pl.BlockSpec((tq, MLA_HEADS * MLA_QW), lambda b, i: (b * nq + i, 0)),
                  pl.BlockSpec((tk_len, MLA_HEADS * MLA_QW), lambda b, i: (b, 0)),
                  pl.BlockSpec((tk_len, MLA_HEADS * MLA_VW), lambda b, i: (b, 0))],
        out_specs=pl.BlockSpec((tq, MLA_HEADS * V_DIM), lambda b, i: (b * nq + i, 0)),
        out_shape=jax.ShapeDtypeStruct((batch * tq_len, MLA_HEADS * V_DIM), out_dtype),
        compiler_params=_cparams(("parallel", "arbitrary")),
        name="mla_attn",
    )(q, k, v)


def _na_kernel(q_ref, k_ref, v_ref, ck_ref, cv_ref, tab_ref, o_ref, *, rows):
    nloc = WIN_R * GRID_W
    pairs = [(j, h) for j in range(NA_ROWS_PER_STEP) for h in range(NA_HEADS)]
    k0s, tab0s = [], []
    for j in range(NA_ROWS_PER_STEP):
        r = pl.program_id(1) * NA_ROWS_PER_STEP + j
        rs = jnp.clip(r - WIN_R // 2, 0, rows - WIN_R)
        k0s.append(pl.multiple_of(rs * GRID_W, GRID_W))
        tab0s.append(rs - r + (WIN_R - 1))

    scores = []
    for j, h in pairs:
        hs = slice(h * NA_DH, (h + 1) * NA_DH)
        qh = q_ref[j * GRID_W:(j + 1) * GRID_W, hs]
        bias = jnp.concatenate([tab_ref[h, tab0s[j] + 2 * p] for p in range(WIN_R // 2)], axis=1)
        scores.append((_bdot_nt(qh, k_ref[pl.ds(k0s[j], nloc), hs]) + bias, _bdot_nt(qh, ck_ref[:, hs])))
    probs = []
    for s_loc, s_ctx in scores:
        m = jnp.maximum(jnp.max(s_loc, axis=1, keepdims=True), jnp.max(s_ctx, axis=1, keepdims=True))
        probs.append((jnp.exp(s_loc - m).astype(BF16), jnp.exp(s_ctx - m).astype(BF16)))
    outs = []
    for (j, h), (p_loc, p_ctx) in zip(pairs, probs):
        vs = slice(h * NA_VW, (h + 1) * NA_VW)
        o_aug = (jnp.dot(p_loc, v_ref[pl.ds(k0s[j], nloc), vs], preferred_element_type=F32)
                 + jnp.dot(p_ctx, cv_ref[:, vs], preferred_element_type=F32))
        outs.append(o_aug / pltpu.roll(o_aug, NA_DH, axis=1))
    lane = lax.broadcasted_iota(jnp.int32, (GRID_W, 2 * NA_DH), 1)
    for j in range(NA_ROWS_PER_STEP):
        for hp in range(NA_HEADS // 2):
            even, odd = outs[j * NA_HEADS + 2 * hp], outs[j * NA_HEADS + 2 * hp + 1]
            o_ref[j * GRID_W:(j + 1) * GRID_W, hp * 2 * NA_DH:(hp + 1) * 2 * NA_DH] = jnp.where(
                lane < NA_DH, even, pltpu.roll(odd, NA_DH, axis=1)).astype(o_ref.dtype)


def _na_bias_table(rpb):
    c = np.arange(GRID_W)
    cs = np.clip(c - WIN_C // 2, 0, GRID_W - WIN_C)
    kc = np.arange(GRID_W)
    valid = (kc[None, :] >= cs[:, None]) & (kc[None, :] < cs[:, None] + WIN_C)
    coff = np.clip(kc[None, :] - c[:, None] + (WIN_C - 1), 0, 2 * WIN_C - 2)
    onehot = (valid[:, :, None] & (coff[:, :, None] == np.arange(2 * WIN_C - 1))).astype(np.float32)
    tab = jnp.einsum('hdo,cko->hdck', rpb.astype(F32), onehot, precision=lax.Precision.HIGHEST)
    tab = tab + np.where(valid, 0.0, NEG_BIG).astype(np.float32)
    return jnp.concatenate([tab[:, :-1], tab[:, 1:]], axis=-1)


def _na_value_layout(v, ones_value):
    v3 = v.reshape(v.shape[:-1] + (NA_HEADS, NA_DH))
    pad = jnp.full(v3.shape[:-1] + (NA_VW - NA_DH,), ones_value, v.dtype)
    return jnp.concatenate([v3, pad], axis=-1).reshape(v.shape[:-1] + (NA_HEADS * NA_VW,))


def _na_attention(zn, col0, ck, cv, tab, batch, seq, past):
    rows = seq // GRID_W
    steps = rows // NA_ROWS_PER_STEP
    tq = NA_ROWS_PER_STEP * GRID_W
    vw = NA_HEADS * NA_VW
    qb = col0 // BRANCH_W
    vb = (col0 + 2 * BRANCH_W) // vw
    return pl.pallas_call(
        functools.partial(_na_kernel, rows=rows),
        grid=(batch, steps),
        in_specs=[pl.BlockSpec((tq, BRANCH_W), lambda b, r: (b * steps + r, qb)),
                  pl.BlockSpec((seq, BRANCH_W), lambda b, r: (b, qb + 1)),
                  pl.BlockSpec((seq, vw), lambda b, r: (b, vb)),
                  pl.BlockSpec((past, BRANCH_W), lambda b, r: (b, 0)),
                  pl.BlockSpec((past, vw), lambda b, r: (b, 0)),
                  pl.BlockSpec(tab.shape, lambda b, r: (0, 0, 0, 0))],
        out_specs=pl.BlockSpec((tq, BRANCH_W), lambda b, r: (b * steps + r, 0)),
        out_shape=jax.ShapeDtypeStruct((batch * seq, BRANCH_W), zn.dtype),
        compiler_params=_cparams(("parallel", "arbitrary")),
        name="na_attn",
    )(zn, zn, zn, ck, cv, tab)


def _rms(x, g):
    return x * lax.rsqrt(jnp.mean(x * x, axis=-1, keepdims=True) + EPS) * g


def _rotate_pairs(slab, cs):
    prod = slab * cs
    both = prod + pltpu.roll(prod, ROPE_DIM, axis=1)
    lane = lax.broadcasted_iota(jnp.int32, both.shape, 1)
    return jnp.where(lane < ROPE_DIM, both, 0.0)


def _mla_prep_kernel(*refs, has_q, norm_kv, in_place):
    if has_q:
        qd_ref, kvd_ref, kpe_ref, cs_ref, qg_ref, wq_ref, kg_ref, wk_ref, wv_ref, q_ref, k_ref, v_ref, ckv_ref = refs
    elif in_place:
        kvd_ref, kpe_ref, cs_ref, kg_ref, wk_ref, wv_ref, _, _, k_ref, v_ref, ckv_ref = refs
    else:
        kvd_ref, kpe_ref, cs_ref, kg_ref, wk_ref, wv_ref, k_ref, v_ref, ckv_ref = refs
    cs = cs_ref[...]
    if has_q:
        scale = (NOPE_DIM + ROPE_DIM) ** -0.5
        qf = _bdot(_rms(qd_ref[...].astype(F32), qg_ref[...]), wq_ref[...]) * scale
        for h in range(MLA_HEADS):
            base = h * MLA_QW
            q_ref[:, base:base + NOPE_DIM] = qf[:, base:base + NOPE_DIM].astype(q_ref.dtype)
            q_ref[:, base + NOPE_DIM:base + MLA_QW] = _rotate_pairs(
                qf[:, base + NOPE_DIM:base + MLA_QW], cs).astype(q_ref.dtype)
    kvd = kvd_ref[...].astype(F32)
    ckv = _rms(kvd, kg_ref[...]) if norm_kv else kvd
    ckv_ref[...] = ckv
    kn = _bdot(ckv, wk_ref[...])
    vv = _bdot(ckv, wv_ref[...]).astype(v_ref.dtype)
    kr = _rotate_pairs(kpe_ref[...].astype(F32), cs).astype(k_ref.dtype)
    ones = jnp.ones((vv.shape[0], MLA_VW - V_DIM), v_ref.dtype)
    for h in range(MLA_HEADS):
        base = h * MLA_QW
        k_ref[:, base:base + NOPE_DIM] = kn[:, h * NOPE_DIM:(h + 1) * NOPE_DIM].astype(k_ref.dtype)
        k_ref[:, base + NOPE_DIM:base + MLA_QW] = kr
        v_ref[:, h * MLA_VW:h * MLA_VW + V_DIM] = vv[:, h * V_DIM:(h + 1) * V_DIM]
        v_ref[:, h * MLA_VW + V_DIM:(h + 1) * MLA_VW] = ones


def _mla_prep(z, cs_tab, wts, seq, has_q, norm_kv, kvd=None, kpe=None, kv_rows=None, kv_row0=0, kv_into=None):
    qg, wq, kg, wk, wv = wts
    ntok = z.shape[0] if z is not None else kvd.shape[0]
    tm = min(MLA_PREP_ROWS, seq)
    per_b = seq // tm
    kv_rows = seq if kv_rows is None else kv_rows
    kv_per_b, kv_blk0 = kv_rows // tm, kv_row0 // tm
    kv_ntok = (ntok // seq) * kv_rows
    tok = lambda i: (i, 0)
    kv_tok = lambda i: ((i // per_b) * kv_per_b + kv_blk0 + i % per_b, 0)
    const = lambda i: (0, 0)
    in_specs, args = [], []
    if has_q:
        in_specs.append(pl.BlockSpec((tm, Q_RANK), lambda i: (i, ZOFF['c_qd'] // Q_RANK)))
        args.append(z)
    if z is not None:
        in_specs += [pl.BlockSpec((tm, KV_RANK), lambda i: (i, ZOFF['c_kvd'] // KV_RANK)),
                     pl.BlockSpec((tm, LANE), lambda i: (i, ZOFF['c_kpe'] // LANE))]
        args += [z, z]
    else:
        in_specs += [pl.BlockSpec((tm, KV_RANK), tok), pl.BlockSpec((tm, LANE), tok)]
        args += [kvd, kpe]
    in_specs.append(pl.BlockSpec((tm, LANE), lambda i: (i % per_b, 0)))
    args.append(cs_tab)
    if has_q:
        in_specs += [pl.BlockSpec((1, Q_RANK), const), pl.BlockSpec(wq.shape, const)]
        args += [qg.reshape(1, Q_RANK), wq]
    in_specs += [pl.BlockSpec((1, KV_RANK), const), pl.BlockSpec(wk.shape, const), pl.BlockSpec(wv.shape, const)]
    args += [kg.reshape(1, KV_RANK), wk, wv]
    aliases = {}
    if kv_into is not None:
        aliases = {len(args): 0, len(args) + 1: 1}
        in_specs += [pl.BlockSpec(memory_space=pl.ANY)] * 2
        args += list(kv_into)
    out_specs = [pl.BlockSpec((tm, MLA_HEADS * MLA_QW), kv_tok), pl.BlockSpec((tm, MLA_HEADS * MLA_VW), kv_tok),
                 pl.BlockSpec((tm, KV_RANK), tok)]
    out_shape = [jax.ShapeDtypeStruct((kv_ntok, MLA_HEADS * MLA_QW), BF16),
                 jax.ShapeDtypeStruct((kv_ntok, MLA_HEADS * MLA_VW), BF16),
                 jax.ShapeDtypeStruct((ntok, KV_RANK), F32)]
    if has_q:
        assert kv_into is None
        out_specs.insert(0, pl.BlockSpec((tm, MLA_HEADS * MLA_QW), tok))
        out_shape.insert(0, jax.ShapeDtypeStruct((ntok, MLA_HEADS * MLA_QW), BF16))
    return pl.pallas_call(
        functools.partial(_mla_prep_kernel, has_q=has_q, norm_kv=norm_kv, in_place=kv_into is not None),
        grid=(ntok // tm,),
        in_specs=in_specs, out_specs=out_specs, out_shape=out_shape,
        input_output_aliases=aliases,
        compiler_params=_cparams(("parallel",)),
        name="mla_prep",
    )(*args)


def _rope_table(seq):
    t = jnp.arange(seq)
    row = (t // GRID_W).astype(F32)
    col = (t % GRID_W).astype(F32)
    n_pair_axis = ROPE_DIM // 4
    inv = 1.0 / (ROPE_THETA ** (jnp.arange(n_pair_axis, dtype=F32) / n_pair_axis))
    ang = jnp.concatenate([row[:, None] * inv, col[:, None] * inv], axis=-1)
    cos, sin = jnp.cos(ang), jnp.sin(ang)
    return jnp.concatenate([cos, cos, -sin, sin], axis=-1)


def _identity_table(seq):
    return jnp.concatenate([jnp.ones((seq, ROPE_DIM), F32), jnp.zeros((seq, ROPE_DIM), F32)], axis=-1)


def _merge_kernel(x_ref, mod_ref, oa_ref, ob_ref, oc_ref, ga_ref, gb_ref, gc_ref, db_ref, dc_ref, dx_ref, dg_ref,
                  pc_ref, px_ref, nc_ref, nx_ref, wm_ref, bm_ref, wb_ref, wo_ref, cw_ref, lg_ref, lb_ref, y_ref,
                  *, per_b):
    tm = x_ref.shape[0]
    ti = pl.program_id(0) % per_b
    x = x_ref[...]
    mod = mod_ref[0]
    h = _modulate(x, mod).astype(BF16)
    gate = mod[:, 2 * D_MODEL:3 * D_MODEL]

    def f32(ref):
        return ref[...].astype(F32)

    sub_rows = min(MERGE_SUB_ROWS, tm)
    subs = [slice(r0, r0 + sub_rows) for r0 in range(0, tm, sub_rows)]
    merge_gates = [[jnp.dot(h[sl], wm_ref[:, n * D_MODEL:(n + 1) * D_MODEL], preferred_element_type=F32)
                    for n in range(N_BRANCH)] for sl in subs]

    u = f32(dc_ref) * f32(dx_ref)
    last = pc_ref.shape[0] - 1
    prev_row = jnp.where(ti > 0, f32(pc_ref)[last:last + 1, :] * f32(px_ref)[last:last + 1, :], 0.0)
    next_row = jnp.where(ti < per_b - 1, f32(nc_ref)[0:1, :] * f32(nx_ref)[0:1, :], 0.0)
    row = lax.broadcasted_iota(jnp.int32, u.shape, 0)
    u_prev = jnp.where(row == 0, prev_row, pltpu.roll(u, 1, axis=0))
    u_next = jnp.where(row == tm - 1, next_row, pltpu.roll(u, tm - 1, axis=0))
    conv = cw_ref[0:1, :] * u_prev + cw_ref[1:2, :] * u + cw_ref[2:3, :] * u_next
    out_d = f32(db_ref) * conv * _silu(f32(dg_ref))

    branches = [b.astype(BF16) for b in (f32(oa_ref) * _silu(f32(ga_ref)), f32(ob_ref) * _silu(f32(gb_ref)),
                                         f32(oc_ref) * _silu(f32(gc_ref)), out_d)]
    proj = [[jnp.dot(branches[n][sl], wb_ref[n], preferred_element_type=F32) for n in range(N_BRANCH)]
            for sl in subs]
    mixed = []
    for i in range(len(subs)):
        acc = jnp.zeros((sub_rows, D_MODEL), F32)
        for n in range(N_BRANCH):
            mg = merge_gates[i][n] + bm_ref[:, n * D_MODEL:(n + 1) * D_MODEL]
            acc = acc + jax.nn.sigmoid(mg) * proj[i][n]
        mixed.append(acc.astype(BF16))
    outs = [jnp.dot(m, wo_ref[...], preferred_element_type=F32) for m in mixed]
    for sl, out in zip(subs, outs):
        r = DEEPNORM_ALPHA * x[sl] + gate * out
        mu = jnp.mean(r, axis=-1, keepdims=True)
        rc = r - mu
        var = jnp.mean(rc * rc, axis=-1, keepdims=True)
        y_ref[sl, :] = rc * lax.rsqrt(var + EPS) * lg_ref[...] + lb_ref[...]


def _merge(x2, mod, z, o_a, o_b, o_c, wts, seq):
    wm, bm, wb, wo, cw, ln_g, ln_b = wts
    ntok = x2.shape[0]
    tm = min(MERGE_ROWS, seq)
    per_b = seq // tm
    nb = mod.shape[0]
    halo_rows = BF16_SUBLANES if z.dtype == BF16 else SUBLANE
    hb = tm // halo_rows
    last_hb = ntok // halo_rows - 1
    mod_idx = (lambda i: (i // per_b, 0, 0)) if nb > 1 else (lambda i: (0, 0, 0))
    tok = lambda i: (i, 0)
    const2 = lambda i: (0, 0)

    def zcol(name):
        blk = ZOFF[name] // BRANCH_W
        return pl.BlockSpec((tm, BRANCH_W), lambda i: (i, blk))

    def halo(name, nxt):
        blk = ZOFF[name] // BRANCH_W
        if nxt:
            return pl.BlockSpec((halo_rows, BRANCH_W), lambda i: (jnp.minimum((i + 1) * hb, last_hb), blk))
        return pl.BlockSpec((halo_rows, BRANCH_W), lambda i: (jnp.maximum(i * hb - 1, 0), blk))

    br = pl.BlockSpec((tm, BRANCH_W), tok)
    in_specs = [pl.BlockSpec((tm, D_MODEL), tok), pl.BlockSpec((1, 1, 3 * D_MODEL), mod_idx),
                br, br, br, zcol('a_g'), zcol('b_g'), zcol('c_g'),
                zcol('d_b'), zcol('d_c'), zcol('d_x'), zcol('d_g'),
                halo('d_c', False), halo('d_x', False), halo('d_c', True), halo('d_x', True),
                pl.BlockSpec(wm.shape, const2), pl.BlockSpec(bm.shape, const2),
                pl.BlockSpec(wb.shape, lambda i: (0, 0, 0)), pl.BlockSpec(wo.shape, const2),
                pl.BlockSpec(cw.shape, const2), pl.BlockSpec((1, D_MODEL), const2),
                pl.BlockSpec((1, D_MODEL), const2)]
    return pl.pallas_call(
        functools.partial(_merge_kernel, per_b=per_b),
        grid=(ntok // tm,),
        in_specs=in_specs,
        out_specs=pl.BlockSpec((tm, D_MODEL), tok),
        out_shape=jax.ShapeDtypeStruct((ntok, D_MODEL), F32),
        compiler_params=_cparams(("parallel",)),
        name="merge_out",
    )(x2, mod, o_a, o_b, o_c, z, z, z, z, z, z, z, z, z, z, z, wm, bm, wb, wo, cw,
      ln_g.reshape(1, D_MODEL), ln_b.reshape(1, D_MODEL))


def _layer_weights(l, w_in, b_in, lb, hg_norm_g, na_rpb, mla_qnorm_g, mla_w_qb, mla_kvnorm_g, mla_w_kvb,
                   conv_w, w_branch, w_out, ln_g, ln_b):
    w_in_l = w_in[l].astype(BF16)
    b_in_l = b_in[l]
    wz = jnp.concatenate(_z_columns(w_in_l), axis=-1)
    bz = jnp.concatenate(_z_columns(b_in_l), axis=-1).reshape(1, ZW)
    wm = w_in_l[:, MERGE_OFF:]
    bm = b_in_l[MERGE_OFF:].reshape(1, N_BRANCH * D_MODEL)

    def ref_cols(a, name):
        return a[..., _REF_OFF[name]:_REF_OFF[name] + _REF_SIZE[name]]

    def latent_columns(a, ones_value):
        return jnp.concatenate(_z_columns(a, _Z_ORDER[:-3]) + [
            ref_cols(a, 'b_q') * NA_DH ** -0.5, ref_cols(a, 'b_k'),
            _na_value_layout(ref_cols(a, 'b_v'), ones_value)], axis=-1)

    wl = latent_columns(w_in_l, 0.0)
    bl = latent_columns(b_in_l, 1.0).reshape(1, -1)
    lb_f, lb_b = lb[0, l], lb[1, l]
    lbp = jnp.stack([jnp.log(lb_f), jnp.log1p(-lb_f), jnp.log(lb_b), jnp.log1p(-lb_b)], axis=0)
    lbp = jnp.transpose(lbp.reshape(4, HG_HEADS, HG_DK), (1, 0, 2))
    wq3 = mla_w_qb[l].reshape(Q_RANK, MLA_HEADS, NOPE_DIM + ROPE_DIM)
    half = NOPE_DIM + ROPE_DIM // 2
    wq = jnp.concatenate([wq3, wq3[..., half:], wq3[..., NOPE_DIM:half]], axis=-1)
    wq = wq.reshape(Q_RANK, MLA_HEADS * MLA_QW).astype(BF16)
    wkv3 = mla_w_kvb[l].reshape(KV_RANK, MLA_HEADS, NOPE_DIM + V_DIM)
    wk = wkv3[..., :NOPE_DIM].reshape(KV_RANK, MLA_HEADS * NOPE_DIM).astype(BF16)
    wv = wkv3[..., NOPE_DIM:].reshape(KV_RANK, MLA_HEADS * V_DIM).astype(BF16)
    return dict(
        wz=wz, bz=bz, wl=wl, bl=bl, lbp=lbp, hg_norm_g=hg_norm_g[l], na_tab=_na_bias_table(na_rpb[l]),
        mla=(mla_qnorm_g[l], wq, mla_kvnorm_g[l], wk, wv),
        merge=(wm, bm, w_branch[l].astype(BF16), w_out[l].astype(BF16), conv_w[l], ln_g[l], ln_b[l]))


def _context_layer(x2, mod, w, batch, seq):
    z = _in_proj(x2, mod, w['wz'], w['bz'], seq, F32)
    s0 = jnp.zeros((batch, 2, HG_HEADS, HG_DV, HG_DK), F32)
    o_a, s_t = _hgrn(z, w['lbp'], w['hg_norm_g'], s0, batch, seq)
    o_b = _ctx_attention(z, batch, seq)
    q, k, v, ckv = _mla_prep(z, _identity_table(seq), w['mla'], seq, True, True)
    o_c = _mla_attention(q, k, v, batch, seq, seq, F32)
    y = _merge(x2, mod, z, o_a, o_b, o_c, w['merge'], seq)

    def zslice(name, width):
        return z[:, ZOFF[name]:ZOFF[name] + width]

    cache = (jnp.swapaxes(s_t, -1, -2),
             zslice('b_k', BRANCH_W).reshape(batch, seq, NA_HEADS, NA_DH),
             zslice('b_v', BRANCH_W).reshape(batch, seq, NA_HEADS, NA_DH),
             ckv.reshape(batch, seq, KV_RANK),
             zslice('c_kpe', ROPE_DIM).reshape(batch, seq, ROPE_DIM))
    return y, cache


def _latent_layer(x2, mod, w, cache, batch, seq):
    s0, na_k, na_v, c_ckv, c_kpe = cache
    past = na_k.shape[1]
    z = _in_proj(x2, mod, w['wl'], w['bl'], seq, BF16)
    o_a, _ = _hgrn(z, w['lbp'], w['hg_norm_g'], jnp.swapaxes(s0, -1, -2), batch, seq)
    o_b = _na_attention(z, ZW_MAIN, na_k.reshape(batch * past, BRANCH_W).astype(BF16),
                        _na_value_layout(na_v.reshape(batch * past, BRANCH_W), 1.0).astype(BF16),
                        w['na_tab'], batch, seq, past)
    tk_len = seq + past
    q, k, v, _ = _mla_prep(z, _rope_table(seq), w['mla'], seq, True, True, kv_rows=tk_len)
    kpe_pad = jnp.pad(c_kpe.reshape(batch * past, ROPE_DIM), ((0, 0), (0, LANE - ROPE_DIM)))
    k, v, _ = _mla_prep(None, _identity_table(past), w['mla'], past, False, False,
                        kvd=c_ckv.reshape(batch * past, KV_RANK), kpe=kpe_pad,
                        kv_rows=tk_len, kv_row0=seq, kv_into=(k, v))
    o_c = _mla_attention(q, k, v, batch, seq, tk_len, BF16)
    return _merge(x2, mod, z, o_a, o_b, o_c, w['merge'], seq)


def kernel(x_prompt, x_sample, state_hgrn, cache_na_k, cache_na_v, cache_mla_ckv, cache_mla_kpe, c, c_ctx,
           w_ada, b_ada, w_in, b_in, hg_lb_logits, hg_norm_g, na_rpb, mla_qnorm_g, mla_w_qb, mla_kvnorm_g,
           mla_w_kvb, conv_w, w_branch, w_out, ln_g, ln_b):
    batch, seq, _ = x_prompt.shape
    dbatch, dseq, _ = x_sample.shape
    lb = jnp.cumsum(jax.nn.softmax(hg_lb_logits.astype(F32), axis=1), axis=1)
    lb = lb - lb[:, :1]
    n_cond = -(-(dbatch + 1) // SUBLANE) * SUBLANE
    cond = jnp.zeros((n_cond, D_MODEL), F32).at[:dbatch].set(c).at[dbatch].set(c_ctx)
    mod = _modulation(cond, w_ada, b_ada)

    y_p = x_prompt.reshape(batch * seq, D_MODEL)
    y_s = x_sample.reshape(dbatch * dseq, D_MODEL)
    caches = []
    for l in range(DEPTH):
        w = _layer_weights(l, w_in, b_in, lb, hg_norm_g, na_rpb, mla_qnorm_g, mla_w_qb, mla_kvnorm_g,
                           mla_w_kvb, conv_w, w_branch, w_out, ln_g, ln_b)
        mod_ctx = mod[l, dbatch:dbatch + 1].reshape(1, 1, 3 * D_MODEL)
        mod_lat = mod[l, :dbatch].reshape(dbatch, 1, 3 * D_MODEL)
        y_p, cache_l = _context_layer(y_p, mod_ctx, w, batch, seq)
        caches.append(cache_l)
        y_s = _latent_layer(y_s, mod_lat, w,
                            (state_hgrn[:, l], cache_na_k[:, l], cache_na_v[:, l],
                             cache_mla_ckv[:, l], cache_mla_kpe[:, l]), dbatch, dseq)
    outs = [jnp.stack([cl[i] for cl in caches], axis=1) for i in range(5)]
    return (y_p.reshape(batch, seq, D_MODEL), y_s.reshape(dbatch, dseq, D_MODEL), *outs)
```
